```python
import jax, jax.numpy as jnp
from jax import lax
import numpy as np

D_MODEL = 1024
BATCH = 16
SEQ = 2048
DEPTH = 2

RG_WIDTH = 1024
RG_BLOCKS = 16
RG_BLOCK_DIM = RG_WIDTH // RG_BLOCKS
CONV_WIDTH = 4
RG_C = 8.0
SB_HEADS = 8
SB_HEAD_DIM = 64
SB_WIDTH = SB_HEADS * SB_HEAD_DIM
FOX_HEADS = 8
FOX_HEAD_DIM = 64
FOX_WIDTH = FOX_HEADS * FOX_HEAD_DIM
N_BRANCHES = 3
D_FF = 2816
Q_BLOCK = 128
N_SUBLAYERS = 3
EPS = 1e-6
IN_SIZES = (RG_WIDTH, RG_WIDTH, 3 * SB_WIDTH, 3 * FOX_WIDTH, FOX_HEADS, N_BRANCHES * D_MODEL)
N_IN = sum(IN_SIZES)

kernel_name = "hybrid_rglru_stickbreak_fox_macaron_adaln"


def _rmsnorm(x, gain):
    x32 = x.astype(jnp.float32)
    y = x32 * lax.rsqrt(jnp.mean(x32 * x32, axis=-1, keepdims=True) + EPS)
    return y.astype(x.dtype) * gain


def _modulate(h, shift, scale):
    return h * (1.0 + scale[:, None, :]) + shift[:, None, :]


def _swiglu(h, w1, w3, w2):
    return (jax.nn.silu(h @ w1) * (h @ w3)) @ w2


def _causal_depthwise_conv(x, w, b):
    y = lax.conv_general_dilated(
        x, w[:, None, :].astype(x.dtype), window_strides=(1,), padding=[(CONV_WIDTH - 1, 0)],
        dimension_numbers=('NWC', 'WIO', 'NWC'), feature_group_count=x.shape[-1])
    return y + b


def _block_diag(x, w, b):
    bsz, slen, _ = x.shape
    xb = x.reshape(bsz, slen, RG_BLOCKS, RG_BLOCK_DIM)
    return jnp.einsum('bsnd,nde->bsne', xb, w).reshape(bsz, slen, RG_WIDTH) + b


def _rg_lru(x, w_a, b_a, w_x, b_x, lam):
    x32 = x.astype(jnp.float32)
    r = jax.nn.sigmoid(_block_diag(x, w_a, b_a).astype(jnp.float32))
    i = jax.nn.sigmoid(_block_diag(x, w_x, b_x).astype(jnp.float32))
    log_a = -RG_C * r * jax.nn.softplus(-lam.astype(jnp.float32))
    a = jnp.exp(log_a)
    u = jnp.sqrt(-jnp.expm1(2.0 * log_a)) * (i * x32)

    def combine(left, right):
        a_l, b_l = left
        a_r, b_r = right
        return a_l * a_r, a_r * b_l + b_r

    _, h = lax.associative_scan(combine, (a, u), axis=1)
    return h.astype(x.dtype)


def _split_qkv(t, n_heads, head_dim):
    bsz, slen, _ = t.shape
    t = t.reshape(bsz, slen, 3, n_heads, head_dim).transpose(2, 0, 3, 1, 4)
    return t[0], t[1], t[2]


def _merge_heads(o):
    bsz, nh, slen, hd = o.shape
    return o.transpose(0, 2, 1, 3).reshape(bsz, slen, nh * hd)


def _stick_breaking_attention(q, k, v):
    slen, hd = q.shape[2], q.shape[3]
    scale = hd ** -0.5
    outs = []
    for blk in range(slen // Q_BLOCK):
        q0, q1 = blk * Q_BLOCK, (blk + 1) * Q_BLOCK
        z = jnp.einsum('bhqd,bhkd->bhqk', q[:, :, q0:q1], k[:, :, :q1],
                       preferred_element_type=jnp.float32) * scale
        t_idx = q0 + jnp.arange(Q_BLOCK)[:, None]
        s_idx = jnp.arange(q1)[None, :]
        strict = s_idx < t_idx
        log_keep = jnp.where(strict, jax.nn.log_sigmoid(-z), 0.0)
        suffix = lax.cumsum(log_keep, axis=3, reverse=True) - log_keep
        w = jnp.where(strict, jnp.exp(jax.nn.log_sigmoid(z) + suffix), 0.0)
        outs.append(jnp.einsum('bhqk,bhkd->bhqd', w.astype(v.dtype), v[:, :, :q1]))
    return jnp.concatenate(outs, axis=2)


def _forgetting_attention(q, k, v, log_f):
    slen, hd = q.shape[2], q.shape[3]
    scale = hd ** -0.5
    cum = lax.cumsum(log_f, axis=2)
    outs = []
    for blk in range(slen // Q_BLOCK):
        q0, q1 = blk * Q_BLOCK, (blk + 1) * Q_BLOCK
        z = jnp.einsum('bhqd,bhkd->bhqk', q[:, :, q0:q1], k[:, :, :q1],
                       preferred_element_type=jnp.float32) * scale
        z = z + cum[:, :, q0:q1, None] - cum[:, :, None, :q1]
        causal = jnp.arange(q1)[None, :] <= (q0 + jnp.arange(Q_BLOCK)[:, None])
        p = jax.nn.softmax(jnp.where(causal, z, -jnp.inf), axis=-1)
        outs.append(jnp.einsum('bhqk,bhkd->bhqd', p.astype(v.dtype), v[:, :, :q1]))
    return jnp.concatenate(outs, axis=2)


def _hybrid_mixer(h, w_in, conv_w, conv_b, rg_wa, rg_ba, rg_wx, rg_bx, rg_lam, fox_bf, merge_b,
                  w_rg, w_sb, w_fox, w_o):
    bsz, slen, _ = h.shape
    proj = h @ w_in
    cuts = np.cumsum(IN_SIZES)[:-1].tolist()
    rg_x, rg_gate, sb_qkv, fox_qkv, fox_f, merge = jnp.split(proj, cuts, axis=-1)
    xa = _causal_depthwise_conv(rg_x, conv_w, conv_b)
    ya = jax.nn.gelu(rg_gate) * _rg_lru(xa, rg_wa, rg_ba, rg_wx, rg_bx, rg_lam)
    q_b, k_b, v_b = _split_qkv(sb_qkv, SB_HEADS, SB_HEAD_DIM)
    yb = _merge_heads(_stick_breaking_attention(q_b, k_b, v_b))
    q_c, k_c, v_c = _split_qkv(fox_qkv, FOX_HEADS, FOX_HEAD_DIM)
    log_f = jax.nn.log_sigmoid((fox_f + fox_bf).astype(jnp.float32)).transpose(0, 2, 1)
    yc = _merge_heads(_forgetting_attention(q_c, k_c, v_c, log_f))
    g = jax.nn.sigmoid(merge + merge_b).reshape(bsz, slen, N_BRANCHES, D_MODEL)
    mixed = g[:, :, 0] * (ya @ w_rg) + g[:, :, 1] * (yb @ w_sb) + g[:, :, 2] * (yc @ w_fox)
    return mixed @ w_o


def setup_inputs(seed: int = 0) -> dict:
    key = jax.random.key(seed)
    ks = jax.random.split(key, 32)
    f32 = jnp.float32
    L, D = DEPTH, D_MODEL

    def nrm(k, shape, fan_in, mult=1.0):
        return jax.random.normal(k, shape, f32) * (mult * fan_in ** -0.5)

    def gain(k, shape):
        return 1.0 + 0.01 * jax.random.normal(k, shape, f32)

    def small(k, shape):
        return 0.01 * jax.random.normal(k, shape, f32)

    a_c = jax.random.uniform(ks[12], (L, RG_WIDTH), f32, 0.9, 0.999)
    a = a_c ** (1.0 / RG_C)
    rg_lam = jnp.log(a) - jnp.log1p(-a)
    return {
        "x": jax.random.normal(ks[0], (BATCH, SEQ, D), f32),
        "c": jax.random.normal(ks[1], (BATCH, D), f32),
        "ffn1_norm": gain(ks[2], (L, D)),
        "ffn1_w1": nrm(ks[3], (L, D, D_FF), D),
        "ffn1_w3": nrm(ks[4], (L, D, D_FF), D),
        "ffn1_w2": nrm(ks[5], (L, D_FF, D), D_FF),
        "mix_norm": gain(ks[6], (L, D)),
        "w_in": nrm(ks[7], (L, D, N_IN), D),
        "conv_w": nrm(ks[8], (L, CONV_WIDTH, RG_WIDTH), CONV_WIDTH),
        "conv_b": small(ks[9], (L, RG_WIDTH)),
        "rg_wa": nrm(ks[10], (L, RG_BLOCKS, RG_BLOCK_DIM, RG_BLOCK_DIM), RG_BLOCK_DIM),
        "rg_ba": small(ks[11], (L, RG_WIDTH)),
        "rg_wx": nrm(ks[13], (L, RG_BLOCKS, RG_BLOCK_DIM, RG_BLOCK_DIM), RG_BLOCK_DIM),
        "rg_bx": small(ks[14], (L, RG_WIDTH)),
        "rg_lam": rg_lam,
        "fox_bf": jax.random.uniform(ks[15], (L, FOX_HEADS), f32, 2.0, 5.0),
        "merge_b": small(ks[16], (L, N_BRANCHES * D)),
        "w_rg": nrm(ks[17], (L, RG_WIDTH, D), RG_WIDTH),
        "w_sb": nrm(ks[18], (L, SB_WIDTH, D), SB_WIDTH),
        "w_fox": nrm(ks[19], (L, FOX_WIDTH, D), FOX_WIDTH),
        "w_o": nrm(ks[20], (L, D, D), D),
        "ffn2_norm": gain(ks[21], (L, D)),
        "ffn2_w1": nrm(ks[22], (L, D, D_FF), D),
        "ffn2_w3": nrm(ks[23], (L, D, D_FF), D),
        "ffn2_w2": nrm(ks[24], (L, D_FF, D), D_FF),
        "ada_w": nrm(ks[25], (L, D, N_SUBLAYERS * 3 * D), D, 0.1),
        "ada_b": small(ks[26], (L, N_SUBLAYERS * 3 * D)),
        "final_norm": gain(ks[27], (D,)),
        "final_ada_w": nrm(ks[28], (D, 2 * D), D, 0.1),
        "final_ada_b": small(ks[29], (2 * D,)),
    }


def reference(x, c, ffn1_norm, ffn1_w1, ffn1_w3, ffn1_w2, mix_norm, w_in, conv_w, conv_b,
              rg_wa, rg_ba, rg_wx, rg_bx, rg_lam, fox_bf, merge_b, w_rg, w_sb, w_fox, w_o,
              ffn2_norm, ffn2_w1, ffn2_w3, ffn2_w2, ada_w, ada_b, final_norm, final_ada_w,
              final_ada_b):
    bsz = x.shape[0]
    c_act = jax.nn.silu(c)
    for l in range(DEPTH):
        mod = (c_act @ ada_w[l] + ada_b[l]).reshape(bsz, N_SUBLAYERS, 3, D_MODEL)
        h = _modulate(_rmsnorm(x, ffn1_norm[l]), mod[:, 0, 0], mod[:, 0, 1])
        x = x + 0.5 * (1.0 + mod[:, 0, 2])[:, None, :] * _swiglu(h, ffn1_w1[l], ffn1_w3[l], ffn1_w2[l])
        h = _modulate(_rmsnorm(x, mix_norm[l]), mod[:, 1, 0], mod[:, 1, 1])
        y = _hybrid_mixer(h, w_in[l], conv_w[l], conv_b[l], rg_wa[l], rg_ba[l], rg_wx[l], rg_bx[l],
                          rg_lam[l], fox_bf[l], merge_b[l], w_rg[l], w_sb[l], w_fox[l], w_o[l])
        x = x + (1.0 + mod[:, 1, 2])[:, None, :] * y
        h = _modulate(_rmsnorm(x, ffn2_norm[l]), mod[:, 2, 0], mod[:, 2, 1])
        x = x + 0.5 * (1.0 + mod[:, 2, 2])[:, None, :] * _swiglu(h, ffn2_w1[l], ffn2_w3[l], ffn2_w2[l])
    fm = (c_act @ final_ada_w + final_ada_b).reshape(bsz, 2, D_MODEL)
    return _modulate(_rmsnorm(x, final_norm), fm[:, 0], fm[:, 1])
```

```python
import functools

import jax
import jax.numpy as jnp
import numpy as np
from jax import lax
from jax.experimental import pallas as pl
from jax.experimental.pallas import tpu as pltpu

F32 = jnp.float32
BF16 = jnp.bfloat16

D_MODEL = 1024
RG_WIDTH = 1024
RG_BLOCK_DIM = 64
CONV_WIDTH = 4
RG_C = 8.0
HEAD_DIM = 64
ATT_WIDTH = 512
N_HEADS = 8
D_FF = 2816
EPS = 1e-6

LANES = 128
SUBLANES = 8
MXU_DIM = 256
FF_CHUNK = MXU_DIM
ATT_BLOCK = 128
RG_GROUP = MXU_DIM
VMEM_LIMIT = 56 * 1024 * 1024


def _params(n_grid):
    return pltpu.CompilerParams(dimension_semantics=("parallel",) * n_grid, vmem_limit_bytes=VMEM_LIMIT)


def _resident(shape):
    nd = len(shape)
    return pl.BlockSpec(shape, lambda *_: (0,) * nd, pipeline_mode=pl.Buffered(1))


def _softplus(y):
    return jnp.maximum(y, 0.0) + jnp.log1p(jnp.exp(-jnp.abs(y)))


def _norm_mod(x, gain, shift, scale):
    y = x * lax.rsqrt(jnp.mean(x * x, axis=-1, keepdims=True) + EPS)
    return (y * gain) * (1.0 + scale) + shift


def _ada_kernel(c_ref, w_ref, b_ref, o_ref):
    c = c_ref[...]
    o_ref[...] = jnp.dot(c * jax.nn.sigmoid(c), w_ref[...], preferred_element_type=F32) + b_ref[...]


def _ada(c, w, b):
    n_l, d, n = w.shape
    bsz = c.shape[0]
    tn = 1024
    return pl.pallas_call(
        _ada_kernel,
        out_shape=jax.ShapeDtypeStruct((n_l, bsz, n), F32),
        grid=(n_l, n // tn),
        in_specs=[
            pl.BlockSpec((bsz, d), lambda l, j: (0, 0)),
            pl.BlockSpec((None, d, tn), lambda l, j: (l, 0, j)),
            pl.BlockSpec((None, 1, tn), lambda l, j: (l, 0, j)),
        ],
        out_specs=pl.BlockSpec((None, bsz, tn), lambda l, j: (l, 0, j)),
        compiler_params=_params(2),
        name="ada",
    )(c, w, b.reshape(n_l, 1, n))


def _ffn_kernel(sub, x_ref, g_ref, mod_ref, w13_ref, w2_ref, o_ref, h_ref, acc_ref):
    x = x_ref[...]
    h = _norm_mod(x, g_ref[...], mod_ref[3 * sub : 3 * sub + 1, :], mod_ref[3 * sub + 1 : 3 * sub + 2, :])
    h_ref[...] = h.astype(BF16)
    acc_ref[...] = jnp.zeros_like(acc_ref)

    def chunk(c, carry):
        hw = jnp.dot(h_ref[...], w13_ref[c], preferred_element_type=F32)
        a = hw[:, :FF_CHUNK]
        act = (a * jax.nn.sigmoid(a)) * hw[:, FF_CHUNK:]
        acc_ref[...] += jnp.dot(act.astype(BF16), w2_ref[c], preferred_element_type=F32)
        return carry

    lax.fori_loop(0, w13_ref.shape[0], chunk, 0)
    gate = mod_ref[3 * sub + 2 : 3 * sub + 3, :]
    o_ref[...] = x + (0.5 * (1.0 + gate)) * acc_ref[...]


def _ffn(x, gain, mod, w13, w2, sub, tm):
    bsz, slen, d = x.shape
    tile = pl.BlockSpec((None, tm, d), lambda b, i: (b, i, 0))
    return pl.pallas_call(
        functools.partial(_ffn_kernel, sub),
        out_shape=jax.ShapeDtypeStruct(x.shape, F32),
        grid=(bsz, slen // tm),
        in_specs=[
            tile,
            pl.BlockSpec((1, d), lambda b, i: (0, 0)),
            pl.BlockSpec((None, 9, d), lambda b, i: (b, 0, 0)),
            _resident(w13.shape),
            _resident(w2.shape),
        ],
        out_specs=tile,
        scratch_shapes=[pltpu.VMEM((tm, d), BF16), pltpu.VMEM((tm, d), F32)],
        compiler_params=_params(2),
        name="ffn",
    )(x, gain, mod, w13, w2)


def _inproj_kernel(x_ref, g_ref, mod_ref, wx_ref, wg_ref, wsb_ref, wfox_ref, wf_ref, wmg_ref,
                   rgx_ref, rgg_ref, sb_ref, fox_ref, f_ref, mg_ref, h_ref):
    h = _norm_mod(x_ref[...], g_ref[...], mod_ref[3:4, :], mod_ref[4:5, :])
    h_ref[...] = h.astype(BF16)
    for w_ref, o_ref in ((wx_ref, rgx_ref), (wg_ref, rgg_ref), (wsb_ref, sb_ref), (wfox_ref, fox_ref),
                         (wf_ref, f_ref), (wmg_ref, mg_ref)):
        n = w_ref.shape[1]
        step = min(n, 2 * MXU_DIM)
        for c0 in range(0, n, step):
            o_ref[:, c0 : c0 + step] = jnp.dot(
                h_ref[...], w_ref[:, c0 : c0 + step], preferred_element_type=F32).astype(o_ref.dtype)


def _inproj(x, gain, mod, ws, tm):
    bsz, slen, d = x.shape
    out_dtypes = (F32, F32, BF16, BF16, F32, BF16)
    row = lambda n: pl.BlockSpec((None, tm, n), lambda b, i: (b, i, 0))
    return pl.pallas_call(
        _inproj_kernel,
        out_shape=[jax.ShapeDtypeStruct((bsz, slen, w.shape[1]), dt) for w, dt in zip(ws, out_dtypes)],
        grid=(bsz, slen // tm),
        in_specs=[row(d), pl.BlockSpec((1, d), lambda b, i: (0, 0)), pl.BlockSpec((None, 9, d), lambda b, i: (b, 0, 0))]
        + [_resident(w.shape) for w in ws],
        out_specs=[row(w.shape[1]) for w in ws],
        scratch_shapes=[pltpu.VMEM((tm, d), BF16)],
        compiler_params=_params(2),
        name="inproj",
    )(x, gain, mod, *ws)


def _fcum_kernel(f_ref, bf_ref, col_ref, row_ref):
    slen = f_ref.shape[0]
    c = -_softplus(-(f_ref[...] + bf_ref[...]))
    t = lax.broadcasted_iota(jnp.int32, c.shape, 0)
    d = 1
    while d < slen:
        c = c + jnp.where(t >= d, pltpu.roll(c, d, 0), 0.0)
        d *= 2
    col_ref[...] = c
    row_ref[...] = c.T


def _fcum(f, bf):
    bsz, slen, n = f.shape
    return pl.pallas_call(
        _fcum_kernel,
        out_shape=[jax.ShapeDtypeStruct((bsz, slen, n), F32), jax.ShapeDtypeStruct((bsz, n, slen), F32)],
        grid=(bsz,),
        in_specs=[pl.BlockSpec((None, slen, n), lambda b: (b, 0, 0)), pl.BlockSpec((1, n), lambda b: (0, 0))],
        out_specs=[pl.BlockSpec((None, slen, n), lambda b: (b, 0, 0)), pl.BlockSpec((None, n, slen), lambda b: (b, 0, 0))],
        compiler_params=_params(1),
        name="fcum",
    )(f, bf)


def _rglru_kernel(x_ref, gt_ref, cw_ref, cb_ref, wa_ref, ba_ref, wx_ref, bx_ref, lam_ref, o_ref,
                  xpad_ref, a_ref, u_ref):
    slen, tc = a_ref.shape
    pad = SUBLANES
    xpad_ref[0:pad, :] = jnp.zeros((pad, tc), F32)
    xpad_ref[pad : pad + slen, :] = x_ref[...]
    sp_lam = _softplus(-lam_ref[...])
    rc = min(256, slen)
    for r0 in range(0, slen, rc):
        xa = cb_ref[...]
        for k in range(CONV_WIDTH):
            xa = xa + cw_ref[k : k + 1, :] * xpad_ref[pl.ds(pad + r0 - (CONV_WIDTH - 1) + k, rc), :]
        xab = xa.astype(BF16)
        for g in range(tc // RG_GROUP):
            cs = slice(g * RG_GROUP, (g + 1) * RG_GROUP)
            r = jax.nn.sigmoid(jnp.dot(xab[:, cs], wa_ref[g], preferred_element_type=F32) + ba_ref[:, cs])
            i = jax.nn.sigmoid(jnp.dot(xab[:, cs], wx_ref[g], preferred_element_type=F32) + bx_ref[:, cs])
            log_a = (-RG_C * r) * sp_lam[:, cs]
            a = jnp.exp(log_a)
            u = jnp.sqrt(-jnp.tanh(log_a) * (a * a + 1.0)) * (i * xa[:, cs])
            a_ref[r0 : r0 + rc, cs] = a
            u_ref[r0 : r0 + rc, cs] = u

    row = lax.broadcasted_iota(jnp.int32, (SUBLANES, tc), 0)

    def scan8(k, carry):
        o = pl.multiple_of(k * SUBLANES, SUBLANES)
        a = a_ref[pl.ds(o, SUBLANES), :]
        u = u_ref[pl.ds(o, SUBLANES), :]
        for d in (1, 2, 4):
            a_s = jnp.where(row >= d, pltpu.roll(a, d, 0), 1.0)
            u_s = jnp.where(row >= d, pltpu.roll(u, d, 0), 0.0)
            u = u + a * u_s
            a = a * a_s
        h = a * carry + u
        u_ref[pl.ds(o, SUBLANES), :] = h
        return jnp.broadcast_to(h[SUBLANES - 1 :, :], (SUBLANES, tc))

    lax.fori_loop(0, slen // SUBLANES, scan8, jnp.zeros((SUBLANES, tc), F32), unroll=8)
    for r0 in range(0, slen, rc):
        gate = jax.nn.gelu(gt_ref[r0 : r0 + rc, :], approximate=True)
        o_ref[r0 : r0 + rc, :] = (gate * u_ref[r0 : r0 + rc, :]).astype(o_ref.dtype)


def _rglru(rgx, rgg, conv_w, conv_b, wa_bd, ba, wx_bd, bx, lam, tc):
    bsz, slen, width = rgx.shape
    tile = pl.BlockSpec((None, slen, tc), lambda b, j: (b, 0, j))
    vec = pl.BlockSpec((1, tc), lambda b, j: (0, j))
    wblk = pl.BlockSpec((tc // RG_GROUP, RG_GROUP, RG_GROUP), lambda b, j: (j, 0, 0))
    return pl.pallas_call(
        _rglru_kernel,
        out_shape=jax.ShapeDtypeStruct(rgx.shape, BF16),
        grid=(bsz, width // tc),
        in_specs=[tile, tile, pl.BlockSpec((CONV_WIDTH, tc), lambda b, j: (0, j)), vec, wblk, vec, wblk, vec, vec],
        out_specs=tile,
        scratch_shapes=[pltpu.VMEM((slen + SUBLANES, tc), F32), pltpu.VMEM((slen, tc), F32), pltpu.VMEM((slen, tc), F32)],
        compiler_params=_params(2),
        name="rglru",
    )(rgx, rgg, conv_w, conv_b, wa_bd, ba, wx_bd, bx, lam)


def _qk(qm, k):
    return lax.dot_general(qm, k, (((1,), (1,)), ((), ())), preferred_element_type=F32)


def _sb_kernel(q_ref, k_ref, v_ref, tri_ref, o_ref):
    tb = ATT_BLOCK
    lm0 = lax.broadcasted_iota(jnp.int32, (tb, LANES), 1) < HEAD_DIM
    strict = lax.broadcasted_iota(jnp.int32, (tb, tb), 1) < lax.broadcasted_iota(jnp.int32, (tb, tb), 0)

    def block(qm, k0, c, acc, diag):
        z = _qk(qm, k_ref[pl.ds(k0, tb), :])
        sp = _softplus(z)
        lk = -sp
        if diag:
            lk = jnp.where(strict, lk, 0.0)
        hi = lk.astype(BF16)
        lo = (lk - hi.astype(F32)).astype(BF16)
        cs = jnp.dot(jnp.concatenate([hi, lo], axis=1), tri_ref[...], preferred_element_type=F32)
        w = jnp.exp((z - sp) + cs[:, :tb] + c)
        if diag:
            w = jnp.where(strict, w, 0.0)
        acc = acc + jnp.dot(w.astype(BF16), v_ref[pl.ds(k0, tb), :], preferred_element_type=F32)
        return c + cs[:, tb:], acc

    def q_block(qi, carry):
        q0 = pl.multiple_of(qi * tb, tb)
        q = q_ref[pl.ds(q0, tb), :] * (HEAD_DIM ** -0.5)
        qa = jnp.where(lm0, q, jnp.zeros_like(q))
        qb = jnp.where(lm0, jnp.zeros_like(q), q)
        zero = jnp.zeros((tb, LANES), F32)
        ca, acca = block(qa, q0, zero, zero, True)
        cb, accb = block(qb, q0, zero, zero, True)

        def k_block(j, st):
            ca, acca, cb, accb = st
            k0 = pl.multiple_of((qi - 1 - j) * tb, tb)
            ca, acca = block(qa, k0, ca, acca, False)
            cb, accb = block(qb, k0, cb, accb, False)
            return ca, acca, cb, accb

        _, acca, _, accb = lax.fori_loop(0, qi, k_block, (ca, acca, cb, accb))
        o_ref[pl.ds(q0, tb), :] = jnp.where(lm0, acca, accb).astype(o_ref.dtype)
        return carry

    lax.fori_loop(0, q_ref.shape[0] // tb, q_block, 0)


def _tri_ones():
    tb = ATT_BLOCK
    t = np.tril(np.ones((tb, tb), np.float32), -1)
    half = np.concatenate([t, np.ones((tb, tb), np.float32)], axis=1)
    return jnp.asarray(np.concatenate([half, half], axis=0), BF16)


def _head_pair_specs(slen, n_pairs):
    return [pl.BlockSpec((None, slen, LANES), lambda b, p, o=o: (b, 0, o * n_pairs + p)) for o in range(3)]


def _sb_attention(qkv):
    bsz, slen, _ = qkv.shape
    n_pairs = ATT_WIDTH // LANES
    return pl.pallas_call(
        _sb_kernel,
        out_shape=jax.ShapeDtypeStruct((bsz, slen, ATT_WIDTH), BF16),
        grid=(bsz, n_pairs),
        in_specs=_head_pair_specs(slen, n_pairs) + [pl.BlockSpec((2 * ATT_BLOCK, 2 * ATT_BLOCK), lambda b, p: (0, 0))],
        out_specs=pl.BlockSpec((None, slen, LANES), lambda b, p: (b, 0, p)),
        compiler_params=_params(2),
        name="sb_attn",
    )(qkv, qkv, qkv, _tri_ones())


def _fox_kernel(q_ref, k_ref, v_ref, col_ref, row_ref, o_ref):
    tb = ATT_BLOCK
    pair = pl.program_id(1)
    lane = lax.broadcasted_iota(jnp.int32, (tb, LANES), 1)
    lm0 = lane < HEAD_DIM
    causal = lax.broadcasted_iota(jnp.int32, (tb, tb), 1) <= lax.broadcasted_iota(jnp.int32, (tb, tb), 0)

    def block(qm, head, ct, first, kj, m, acc, diag):
        k0 = pl.multiple_of(kj * tb, tb)
        s = _qk(qm, k_ref[pl.ds(k0, tb), :]) - row_ref[head, kj]
        if diag:
            s = jnp.where(causal, s, -jnp.inf)
        m_new = jnp.maximum(m, jnp.max(s, axis=1, keepdims=True) + ct)
        p = jnp.exp(s + (ct - m_new))
        v = v_ref[pl.ds(k0, tb), :]
        one = jnp.ones_like(v)
        vm = jnp.where(lm0, v, one) if first else jnp.where(lm0, one, v)
        acc = jnp.exp(m - m_new) * acc + jnp.dot(p.astype(BF16), vm, preferred_element_type=F32)
        return m_new, acc

    def q_block(qi, carry):
        q0 = pl.multiple_of(qi * tb, tb)
        q = q_ref[pl.ds(q0, tb), :] * (HEAD_DIM ** -0.5)
        qa = jnp.where(lm0, q, jnp.zeros_like(q))
        qb = jnp.where(lm0, jnp.zeros_like(q), q)
        col = col_ref[pl.ds(q0, tb), :]
        cta = jnp.sum(jnp.where(lane == 2 * pair, col, 0.0), axis=1, keepdims=True)
        ctb = jnp.sum(jnp.where(lane == 2 * pair + 1, col, 0.0), axis=1, keepdims=True)

        def k_block(kj, st):
            ma, acca, mb, accb = st
            ma, acca = block(qa, 2 * pair, cta, True, kj, ma, acca, False)
            mb, accb = block(qb, 2 * pair + 1, ctb, False, kj, mb, accb, False)
            return ma, acca, mb, accb

        neg = jnp.full((tb, 1), -jnp.inf, F32)
        zero = jnp.zeros((tb, LANES), F32)
        ma, acca, mb, accb = lax.fori_loop(0, qi, k_block, (neg, zero, neg, zero))
        _, acca = block(qa, 2 * pair, cta, True, qi, ma, acca, True)
        _, accb = block(qb, 2 * pair + 1, ctb, False, qi, mb, accb, True)
        out = jnp.where(lm0, acca / pltpu.roll(acca, HEAD_DIM, 1), accb / pltpu.roll(accb, HEAD_DIM, 1))
        o_ref[pl.ds(q0, tb), :] = out.astype(o_ref.dtype)
        return carry

    lax.fori_loop(0, q_ref.shape[0] // tb, q_block, 0)


def _fox_attention(qkv, cum_col, cum_row):
    bsz, slen, _ = qkv.shape
    n_pairs = ATT_WIDTH // LANES
    return pl.pallas_call(
        _fox_kernel,
        out_shape=jax.ShapeDtypeStruct((bsz, slen, ATT_WIDTH), BF16),
        grid=(bsz, n_pairs),
        in_specs=_head_pair_specs(slen, n_pairs) + [
            pl.BlockSpec((None, slen, LANES), lambda b, p: (b, 0, 0)),
            pl.BlockSpec((None,) + cum_row.shape[1:], lambda b, p: (b, 0, 0, 0, 0)),
        ],
        out_specs=pl.BlockSpec((None, slen, LANES), lambda b, p: (b, 0, p)),
        compiler_params=_params(2),
        name="fox_attn",
    )(qkv, qkv, qkv, cum_col, cum_row)


def _mix_kernel(x_ref, ya_ref, yb_ref, yc_ref, mg_ref, mb_ref, mod_ref, wrg_ref, wsb_ref, wfox_ref, wo_ref, o_ref):
    d = x_ref.shape[1]
    mixed = None
    for n, (y_ref, w_ref) in enumerate(((ya_ref, wrg_ref), (yb_ref, wsb_ref), (yc_ref, wfox_ref))):
        cs = slice(n * d, (n + 1) * d)
        g = jax.nn.sigmoid(mg_ref[:, cs].astype(F32) + mb_ref[:, cs])
        t = g * jnp.dot(y_ref[...], w_ref[...], preferred_element_type=F32)
        mixed = t if mixed is None else mixed + t
    y = jnp.dot(mixed.astype(BF16), wo_ref[...], preferred_element_type=F32)
    o_ref[...] = x_ref[...] + (1.0 + mod_ref[5:6, :]) * y


def _mix(x, ya, yb, yc, mg, merge_b, mod, w_rg, w_sb, w_fox, w_o, tm):
    bsz, slen, d = x.shape
    row = lambda n: pl.BlockSpec((None, tm, n), lambda b, i: (b, i, 0))
    return pl.pallas_call(
        _mix_kernel,
        out_shape=jax.ShapeDtypeStruct(x.shape, F32),
        grid=(bsz, slen // tm),
        in_specs=[row(d), row(ya.shape[2]), row(yb.shape[2]), row(yc.shape[2]), row(mg.shape[2]),
                  pl.BlockSpec((1, mg.shape[2]), lambda b, i: (0, 0)), pl.BlockSpec((None, 9, d), lambda b, i: (b, 0, 0)),
                  _resident(w_rg.shape), _resident(w_sb.shape), _resident(w_fox.shape), _resident(w_o.shape)],
        out_specs=row(d),
        compiler_params=_params(2),
        name="mix",
    )(x, ya, yb, yc, mg, merge_b, mod, w_rg, w_sb, w_fox, w_o)


def _final_kernel(x_ref, g_ref, fm_ref, o_ref):
    o_ref[...] = _norm_mod(x_ref[...], g_ref[...], fm_ref[0:1, :], fm_ref[1:2, :])


def _final(x, gain, fm, tm):
    bsz, slen, d = x.shape
    tile = pl.BlockSpec((None, tm, d), lambda b, i: (b, i, 0))
    return pl.pallas_call(
        _final_kernel,
        out_shape=jax.ShapeDtypeStruct(x.shape, F32),
        grid=(bsz, slen // tm),
        in_specs=[tile, pl.BlockSpec((1, d), lambda b, i: (0, 0)), pl.BlockSpec((None, 2, d), lambda b, i: (b, 0, 0))],
        out_specs=tile,
        compiler_params=_params(2),
        name="final_norm",
    )(x, gain, fm)


def _ffn_weights(w1, w3, w2):
    d, ff = w1.shape
    nc = ff // FF_CHUNK
    w13 = jnp.stack([w1.reshape(d, nc, FF_CHUNK), w3.reshape(d, nc, FF_CHUNK)], axis=2)
    w13 = w13.transpose(1, 0, 2, 3).reshape(nc, d, 2 * FF_CHUNK)
    return w13.astype(BF16), w2.reshape(nc, FF_CHUNK, d).astype(BF16)


def _block_diag(w):
    n, bd, _ = w.shape
    per = RG_GROUP // bd
    eye = jnp.eye(per, dtype=w.dtype)
    wg = w.reshape(n // per, per, bd, bd)
    return jnp.einsum("gpde,pq->gpdqe", wg, eye).reshape(n // per, per * bd, per * bd).astype(BF16)


def kernel(x, c, ffn1_norm, ffn1_w1, ffn1_w3, ffn1_w2, mix_norm, w_in, conv_w, conv_b, rg_wa, rg_ba, rg_wx, rg_bx, rg_lam, fox_bf, merge_b, w_rg, w_sb, w_fox, w_o, ffn2_norm, ffn2_w1, ffn2_w3, ffn2_w2, ada_w, ada_b, final_norm, final_ada_w, final_ada_b):
    bsz, slen, d = x.shape
    depth = ada_w.shape[0]
    tm = min(512, slen)
    tc = 512
    mods = _ada(c, ada_w, ada_b).reshape(depth, bsz, 9, d)
    fm = _ada(c, final_ada_w[None], final_ada_b[None]).reshape(bsz, 2, d)
    cuts = np.cumsum((RG_WIDTH, RG_WIDTH, 3 * ATT_WIDTH, 3 * ATT_WIDTH, N_HEADS)).tolist()
    row = lambda v: v.reshape(1, -1)
    for l in range(depth):
        mod = mods[l]
        x = _ffn(x, row(ffn1_norm[l]), mod, *_ffn_weights(ffn1_w1[l], ffn1_w3[l], ffn1_w2[l]), 0, tm)
        wl = w_in[l].astype(BF16)
        w_f = jnp.pad(wl[:, cuts[3] : cuts[4]], ((0, 0), (0, LANES - N_HEADS)))
        ws = (wl[:, : cuts[0]], wl[:, cuts[0] : cuts[1]], wl[:, cuts[1] : cuts[2]], wl[:, cuts[2] : cuts[3]], w_f, wl[:, cuts[4] :])
        rgx, rgg, sb_qkv, fox_qkv, f, mg = _inproj(x, row(mix_norm[l]), mod, ws, tm)
        cum_col, cum_row = _fcum(f, jnp.pad(row(fox_bf[l]), ((0, 0), (0, LANES - N_HEADS))))
        cum_row = cum_row[:, :N_HEADS].reshape(bsz, N_HEADS, slen // ATT_BLOCK, 1, ATT_BLOCK)
        ya = _rglru(rgx, rgg, conv_w[l], row(conv_b[l]), _block_diag(rg_wa[l]), row(rg_ba[l]), _block_diag(rg_wx[l]),
                    row(rg_bx[l]), row(rg_lam[l]), tc)
        yb = _sb_attention(sb_qkv)
        yc = _fox_attention(fox_qkv, cum_col, cum_row)
        x = _mix(x, ya, yb, yc, mg, row(merge_b[l]), mod, w_rg[l].astype(BF16), w_sb[l].astype(BF16),
                 w_fox[l].astype(BF16), w_o[l].astype(BF16), tm)
        x = _ffn(x, row(ffn2_norm[l]), mod, *_ffn_weights(ffn2_w1[l], ffn2_w3[l], ffn2_w2[l]), 2, tm)
    return _final(x, row(final_norm), fm, tm)
```

```python
import functools

import jax
import jax.numpy as jnp
import numpy as np
from jax import lax
from jax.experimental import pallas as pl
from jax.experimental.pallas import tpu as pltpu

F32 = jnp.float32
BF16 = jnp.bfloat16

D_MODEL = 1024
RG_WIDTH = 1024
RG_BLOCK_DIM = 64
CONV_WIDTH = 4
RG_C = 8.0
HEAD_DIM = 64
ATT_WIDTH = 512
N_HEADS = 8
D_FF = 2816
EPS = 1e-6

LANES = 128
SUBLANES = 8
MXU_DIM = 256
FF_CHUNK = MXU_DIM
ATT_BLOCK = 128
ATT_Q_ROWS = 512
RG_GROUP = MXU_DIM
VMEM_LIMIT = 56 * 1024 * 1024


def _params(n_grid):
    return pltpu.CompilerParams(dimension_semantics=("parallel",) * n_grid, vmem_limit_bytes=VMEM_LIMIT)


def _resident(shape):
    nd = len(shape)
    return pl.BlockSpec(shape, lambda *_: (0,) * nd, pipeline_mode=pl.Buffered(1))


def _softplus(y):
    return jnp.maximum(y, 0.0) + jnp.log1p(jnp.exp(-jnp.abs(y)))


def _softplus_sum(y):
    return jnp.maximum(y, 0.0) + jnp.log(1.0 + jnp.exp(-jnp.abs(y)))


def _norm_mod(x, gain, shift, scale):
    y = x * lax.rsqrt(jnp.mean(x * x, axis=-1, keepdims=True) + EPS)
    return (y * gain) * (1.0 + scale) + shift


def _ada_kernel(c_ref, w_ref, b_ref, o_ref):
    c = c_ref[...]
    o_ref[...] = jnp.dot(c * jax.nn.sigmoid(c), w_ref[...], preferred_element_type=F32) + b_ref[...]


def _ada(c, w, b):
    n_l, d, n = w.shape
    bsz = c.shape[0]
    tn = 1024
    return pl.pallas_call(
        _ada_kernel,
        out_shape=jax.ShapeDtypeStruct((n_l, bsz, n), F32),
        grid=(n_l, n // tn),
        in_specs=[
            pl.BlockSpec((bsz, d), lambda l, j: (0, 0)),
            pl.BlockSpec((None, d, tn), lambda l, j: (l, 0, j)),
            pl.BlockSpec((None, 1, tn), lambda l, j: (l, 0, j)),
        ],
        out_specs=pl.BlockSpec((None, bsz, tn), lambda l, j: (l, 0, j)),
        compiler_params=_params(2),
        name="ada",
    )(c, w, b.reshape(n_l, 1, n))


def _ffn_kernel(sub, x_ref, g_ref, mod_ref, w13_ref, w2_ref, o_ref, h_ref, acc_ref):
    x = x_ref[...]
    h = _norm_mod(x, g_ref[...], mod_ref[3 * sub : 3 * sub + 1, :], mod_ref[3 * sub + 1 : 3 * sub + 2, :])
    h_ref[...] = h.astype(BF16)
    acc_ref[...] = jnp.zeros_like(acc_ref)

    def chunk(c, carry):
        hw = jnp.dot(h_ref[...], w13_ref[c], preferred_element_type=F32)
        a = hw[:, :FF_CHUNK]
        act = (a * jax.nn.sigmoid(a)) * hw[:, FF_CHUNK:]
        acc_ref[...] += jnp.dot(act.astype(BF16), w2_ref[c], preferred_element_type=F32)
        return carry

    lax.fori_loop(0, w13_ref.shape[0], chunk, 0)
    gate = mod_ref[3 * sub + 2 : 3 * sub + 3, :]
    o_ref[...] = x + (0.5 * (1.0 + gate)) * acc_ref[...]


def _ffn(x, gain, mod, w13, w2, sub, tm):
    bsz, slen, d = x.shape
    tile = pl.BlockSpec((None, tm, d), lambda b, i: (b, i, 0))
    return pl.pallas_call(
        functools.partial(_ffn_kernel, sub),
        out_shape=jax.ShapeDtypeStruct(x.shape, F32),
        grid=(bsz, slen // tm),
        in_specs=[
            tile,
            pl.BlockSpec((1, d), lambda b, i: (0, 0)),
            pl.BlockSpec((None, 9, d), lambda b, i: (b, 0, 0)),
            _resident(w13.shape),
            _resident(w2.shape),
        ],
        out_specs=tile,
        scratch_shapes=[pltpu.VMEM((tm, d), BF16), pltpu.VMEM((tm, d), F32)],
        compiler_params=_params(2),
        name="ffn",
    )(x, gain, mod, w13, w2)


def _inproj_kernel(x_ref, g_ref, mod_ref, wx_ref, wg_ref, wsb_ref, wfox_ref, wf_ref, wmg_ref,
                   rgx_ref, rgg_ref, sb_ref, fox_ref, f_ref, mg_ref, h_ref):
    h = _norm_mod(x_ref[...], g_ref[...], mod_ref[3:4, :], mod_ref[4:5, :])
    h_ref[...] = h.astype(BF16)
    for w_ref, o_ref in ((wx_ref, rgx_ref), (wg_ref, rgg_ref), (wsb_ref, sb_ref), (wfox_ref, fox_ref),
                         (wf_ref, f_ref), (wmg_ref, mg_ref)):
        n = w_ref.shape[1]
        step = min(n, 2 * MXU_DIM)
        for c0 in range(0, n, step):
            o_ref[:, c0 : c0 + step] = jnp.dot(
                h_ref[...], w_ref[:, c0 : c0 + step], preferred_element_type=F32).astype(o_ref.dtype)


def _inproj(x, gain, mod, ws, tm):
    bsz, slen, d = x.shape
    out_dtypes = (F32, F32, BF16, BF16, F32, BF16)
    row = lambda n: pl.BlockSpec((None, tm, n), lambda b, i: (b, i, 0))
    return pl.pallas_call(
        _inproj_kernel,
        out_shape=[jax.ShapeDtypeStruct((bsz, slen, w.shape[1]), dt) for w, dt in zip(ws, out_dtypes)],
        grid=(bsz, slen // tm),
        in_specs=[row(d), pl.BlockSpec((1, d), lambda b, i: (0, 0)), pl.BlockSpec((None, 9, d), lambda b, i: (b, 0, 0))]
        + [_resident(w.shape) for w in ws],
        out_specs=[row(w.shape[1]) for w in ws],
        scratch_shapes=[pltpu.VMEM((tm, d), BF16)],
        compiler_params=_params(2),
        name="inproj",
    )(x, gain, mod, *ws)


def _fcum_kernel(f_ref, bf_ref, col_ref, row_ref):
    slen = f_ref.shape[0]
    c = -_softplus(-(f_ref[...] + bf_ref[...]))
    t = lax.broadcasted_iota(jnp.int32, c.shape, 0)
    d = 1
    while d < slen:
        c = c + jnp.where(t >= d, pltpu.roll(c, d, 0), 0.0)
        d *= 2
    col_ref[...] = c
    row_ref[...] = c.T


def _fcum(f, bf):
    bsz, slen, n = f.shape
    return pl.pallas_call(
        _fcum_kernel,
        out_shape=[jax.ShapeDtypeStruct((bsz, slen, n), F32), jax.ShapeDtypeStruct((bsz, n, slen), F32)],
        grid=(bsz,),
        in_specs=[pl.BlockSpec((None, slen, n), lambda b: (b, 0, 0)), pl.BlockSpec((1, n), lambda b: (0, 0))],
        out_specs=[pl.BlockSpec((None, slen, n), lambda b: (b, 0, 0)), pl.BlockSpec((None, n, slen), lambda b: (b, 0, 0))],
        compiler_params=_params(1),
        name="fcum",
    )(f, bf)


def _rglru_kernel(x_ref, gt_ref, cw_ref, cb_ref, wa_ref, ba_ref, wx_ref, bx_ref, lam_ref, o_ref,
                  xpad_ref, a_ref, u_ref):
    slen, tc = a_ref.shape
    pad = SUBLANES
    xpad_ref[0:pad, :] = jnp.zeros((pad, tc), F32)
    xpad_ref[pad : pad + slen, :] = x_ref[...]
    sp_lam = _softplus(-lam_ref[...])
    rc = min(256, slen)
    for r0 in range(0, slen, rc):
        xa = cb_ref[...]
        for k in range(CONV_WIDTH):
            xa = xa + cw_ref[k : k + 1, :] * xpad_ref[pl.ds(pad + r0 - (CONV_WIDTH - 1) + k, rc), :]
        xab = xa.astype(BF16)
        for g in range(tc // RG_GROUP):
            cs = slice(g * RG_GROUP, (g + 1) * RG_GROUP)
            r = jax.nn.sigmoid(jnp.dot(xab[:, cs], wa_ref[g], preferred_element_type=F32) + ba_ref[:, cs])
            i = jax.nn.sigmoid(jnp.dot(xab[:, cs], wx_ref[g], preferred_element_type=F32) + bx_ref[:, cs])
            log_a = (-RG_C * r) * sp_lam[:, cs]
            a = jnp.exp(log_a)
            u = jnp.sqrt(-jnp.tanh(log_a) * (a * a + 1.0)) * (i * xa[:, cs])
            a_ref[r0 : r0 + rc, cs] = a
            u_ref[r0 : r0 + rc, cs] = u

    row = lax.broadcasted_iota(jnp.int32, (SUBLANES, tc), 0)

    def scan8(k, carry):
        o = pl.multiple_of(k * SUBLANES, SUBLANES)
        a = a_ref[pl.ds(o, SUBLANES), :]
        u = u_ref[pl.ds(o, SUBLANES), :]
        for d in (1, 2, 4):
            a_s = jnp.where(row >= d, pltpu.roll(a, d, 0), 1.0)
            u_s = jnp.where(row >= d, pltpu.roll(u, d, 0), 0.0)
            u = u + a * u_s
            a = a * a_s
        h = a * carry + u
        u_ref[pl.ds(o, SUBLANES), :] = h
        return jnp.broadcast_to(h[SUBLANES - 1 :, :], (SUBLANES, tc))

    lax.fori_loop(0, slen // SUBLANES, scan8, jnp.zeros((SUBLANES, tc), F32), unroll=8)
    for r0 in range(0, slen, rc):
        gate = jax.nn.gelu(gt_ref[r0 : r0 + rc, :], approximate=True)
        o_ref[r0 : r0 + rc, :] = (gate * u_ref[r0 : r0 + rc, :]).astype(o_ref.dtype)


def _rglru(rgx, rgg, conv_w, conv_b, wa_bd, ba, wx_bd, bx, lam, tc):
    bsz, slen, width = rgx.shape
    tile = pl.BlockSpec((None, slen, tc), lambda b, j: (b, 0, j))
    vec = pl.BlockSpec((1, tc), lambda b, j: (0, j))
    wblk = pl.BlockSpec((tc // RG_GROUP, RG_GROUP, RG_GROUP), lambda b, j: (j, 0, 0))
    return pl.pallas_call(
        _rglru_kernel,
        out_shape=jax.ShapeDtypeStruct(rgx.shape, BF16),
        grid=(bsz, width // tc),
        in_specs=[tile, tile, pl.BlockSpec((CONV_WIDTH, tc), lambda b, j: (0, j)), vec, wblk, vec, wblk, vec, vec],
        out_specs=tile,
        scratch_shapes=[pltpu.VMEM((slen + SUBLANES, tc), F32), pltpu.VMEM((slen, tc), F32), pltpu.VMEM((slen, tc), F32)],
        compiler_params=_params(2),
        name="rglru",
    )(rgx, rgg, conv_w, conv_b, wa_bd, ba, wx_bd, bx, lam)


def _qk(qm, k):
    return lax.dot_general(qm, k, (((1,), (1,)), ((), ())), preferred_element_type=F32)


def _masked_q(q_ref, qm_ref, q0, tq):
    lm0 = lax.broadcasted_iota(jnp.int32, (tq, LANES), 1) < HEAD_DIM
    q = q_ref[pl.ds(q0, tq), :] * (HEAD_DIM ** -0.5)
    qm_ref[0] = jnp.where(lm0, q, jnp.zeros_like(q))
    qm_ref[1] = jnp.where(lm0, jnp.zeros_like(q), q)
    return lm0


def _sb_kernel(tq, q_ref, k_ref, v_ref, tri_ref, o_ref, qm_ref, c_ref, acc_ref):
    tk = ATT_BLOCK
    band = tq // tk

    strict = lax.broadcasted_iota(jnp.int32, (tk, tk), 1) < lax.broadcasted_iota(jnp.int32, (tk, tk), 0)

    def step(r_lo, k0, diag):
        chunks = [(h, r) for h in range(2) for r in range(r_lo, tq, tk)]
        kb = k_ref[pl.ds(k0, tk), :]
        vb = v_ref[pl.ds(k0, tk), :]
        zs = [_qk(qm_ref[h, r : r + tk, :], kb) for h, r in chunks]
        mids = []
        for (h, r), z in zip(chunks, zs):
            sp = _softplus_sum(z)
            nk = jnp.where(strict, sp, 0.0) if (diag and r == r_lo) else sp
            hi = nk.astype(BF16)
            lo = (nk - hi.astype(F32)).astype(BF16)
            cs = jnp.dot(jnp.concatenate([hi, lo], axis=1), tri_ref[...], preferred_element_type=F32)
            mids.append((z - sp, cs))
        for (h, r), (lw, cs) in zip(chunks, mids):
            w = jnp.exp(lw - cs[:, :tk] - c_ref[h, r : r + tk, :])
            if diag and r == r_lo:
                w = jnp.where(strict, w, 0.0)
            acc_ref[h, r : r + tk, :] += jnp.dot(w.astype(BF16), vb, preferred_element_type=F32)
            c_ref[h, r : r + tk, :] += cs[:, tk:]

    def q_block(qi, carry):
        q0 = pl.multiple_of(qi * tq, tq)
        lm0 = _masked_q(q_ref, qm_ref, q0, tq)
        c_ref[...] = jnp.zeros_like(c_ref)
        acc_ref[...] = jnp.zeros_like(acc_ref)
        for d in reversed(range(band)):
            step(d * tk, q0 + d * tk, True)

        def k_block(j, carry):
            step(0, pl.multiple_of((qi * band - 1 - j) * tk, tk), False)
            return carry

        lax.fori_loop(0, qi * band, k_block, 0)
        o_ref[pl.ds(q0, tq), :] = jnp.where(lm0, acc_ref[0], acc_ref[1]).astype(o_ref.dtype)
        return carry

    lax.fori_loop(0, q_ref.shape[0] // tq, q_block, 0)


def _tri_ones():
    tb = ATT_BLOCK
    t = np.tril(np.ones((tb, tb), np.float32), -1)
    half = np.concatenate([t, np.ones((tb, tb), np.float32)], axis=1)
    return jnp.asarray(np.concatenate([half, half], axis=0), BF16)


def _head_pair_specs(slen, n_pairs):
    return [pl.BlockSpec((None, slen, LANES), lambda b, p, o=o: (b, 0, o * n_pairs + p)) for o in range(3)]


def _sb_attention(qkv):
    bsz, slen, _ = qkv.shape
    n_pairs = ATT_WIDTH // LANES
    tq = min(ATT_Q_ROWS, slen)
    return pl.pallas_call(
        functools.partial(_sb_kernel, tq),
        out_shape=jax.ShapeDtypeStruct((bsz, slen, ATT_WIDTH), BF16),
        grid=(bsz, n_pairs),
        in_specs=_head_pair_specs(slen, n_pairs) + [pl.BlockSpec((2 * ATT_BLOCK, 2 * ATT_BLOCK), lambda b, p: (0, 0))],
        out_specs=pl.BlockSpec((None, slen, LANES), lambda b, p: (b, 0, p)),
        scratch_shapes=[pltpu.VMEM((2, tq, LANES), BF16), pltpu.VMEM((2, tq, LANES), F32), pltpu.VMEM((2, tq, LANES), F32)],
        compiler_params=_params(2),
        name="sb_attn",
    )(qkv, qkv, qkv, _tri_ones())


def _fox_kernel(tq, q_ref, k_ref, v_ref, col_ref, row_ref, o_ref, qm_ref, ct_ref, m_ref, acc_ref):
    tk = ATT_BLOCK
    band = tq // tk
    pair = pl.program_id(1)
    lm0k = lax.broadcasted_iota(jnp.int32, (tk, LANES), 1) < HEAD_DIM

    causal = lax.broadcasted_iota(jnp.int32, (tk, tk), 1) <= lax.broadcasted_iota(jnp.int32, (tk, tk), 0)

    def step(r_lo, kj, diag):
        chunks = [(h, r) for h in range(2) for r in range(r_lo, tq, tk)]
        k0 = pl.multiple_of(kj * tk, tk)
        kb = k_ref[pl.ds(k0, tk), :]
        v = v_ref[pl.ds(k0, tk), :]
        one = jnp.ones_like(v)
        vm = (jnp.where(lm0k, v, one), jnp.where(lm0k, one, v))
        zs = [_qk(qm_ref[h, r : r + tk, :], kb) for h, r in chunks]
        for (h, r), z in zip(chunks, zs):
            s = z - row_ref[2 * pair + h, kj]
            if diag and r == r_lo:
                s = jnp.where(causal, s, -jnp.inf)
            ct = ct_ref[h, r : r + tk, :]
            m = m_ref[h, r : r + tk, :]
            m_new = jnp.maximum(m, jnp.max(s, axis=1, keepdims=True) + ct)
            p = jnp.exp(s + (ct - m_new))
            acc_ref[h, r : r + tk, :] = jnp.exp(m - m_new) * acc_ref[h, r : r + tk, :] + jnp.dot(
                p.astype(BF16), vm[h], preferred_element_type=F32)
            m_ref[h, r : r + tk, :] = m_new

    def q_block(qi, carry):
        q0 = pl.multiple_of(qi * tq, tq)
        lm0 = _masked_q(q_ref, qm_ref, q0, tq)
        col = col_ref[pl.ds(q0, tq), :]
        lane = lax.broadcasted_iota(jnp.int32, (tq, LANES), 1)
        for h in range(2):
            ct = jnp.sum(jnp.where(lane == 2 * pair + h, col, 0.0), axis=1, keepdims=True)
            ct_ref[h] = jnp.broadcast_to(ct, (tq, LANES))
        m_ref[...] = jnp.full(m_ref.shape, -jnp.inf, F32)
        acc_ref[...] = jnp.zeros_like(acc_ref)
        for d in range(band):
            step(d * tk, qi * band + d, True)

        def k_block(kj, carry):
            step(0, kj, False)
            return carry

        lax.fori_loop(0, qi * band, k_block, 0)
        acca = acc_ref[0]
        accb = acc_ref[1]
        out = jnp.where(lm0, acca / pltpu.roll(acca, HEAD_DIM, 1), accb / pltpu.roll(accb, HEAD_DIM, 1))
        o_ref[pl.ds(q0, tq), :] = out.astype(o_ref.dtype)
        return carry

    lax.fori_loop(0, q_ref.shape[0] // tq, q_block, 0)


def _fox_attention(qkv, cum_col, cum_row):
    bsz, slen, _ = qkv.shape
    n_pairs = ATT_WIDTH // LANES
    tq = min(ATT_Q_ROWS, slen)
    state = pltpu.VMEM((2, tq, LANES), F32)
    return pl.pallas_call(
        functools.partial(_fox_kernel, tq),
        out_shape=jax.ShapeDtypeStruct((bsz, slen, ATT_WIDTH), BF16),
        grid=(bsz, n_pairs),
        in_specs=_head_pair_specs(slen, n_pairs) + [
            pl.BlockSpec((None, slen, LANES), lambda b, p: (b, 0, 0)),
            pl.BlockSpec((None,) + cum_row.shape[1:], lambda b, p: (b, 0, 0, 0, 0)),
        ],
        out_specs=pl.BlockSpec((None, slen, LANES), lambda b, p: (b, 0, p)),
        scratch_shapes=[pltpu.VMEM((2, tq, LANES), BF16), state, state, state],
        compiler_params=_params(2),
        name="fox_attn",
    )(qkv, qkv, qkv, cum_col, cum_row)


def _mix_kernel(x_ref, ya_ref, yb_ref, yc_ref, mg_ref, mb_ref, mod_ref, wrg_ref, wsb_ref, wfox_ref, wo_ref, o_ref):
    d = x_ref.shape[1]
    mixed = None
    for n, (y_ref, w_ref) in enumerate(((ya_ref, wrg_ref), (yb_ref, wsb_ref), (yc_ref, wfox_ref))):
        cs = slice(n * d, (n + 1) * d)
        g = jax.nn.sigmoid(mg_ref[:, cs].astype(F32) + mb_ref[:, cs])
        t = g * jnp.dot(y_ref[...], w_ref[...], preferred_element_type=F32)
        mixed = t if mixed is None else mixed + t
    y = jnp.dot(mixed.astype(BF16), wo_ref[...], preferred_element_type=F32)
    o_ref[...] = x_ref[...] + (1.0 + mod_ref[5:6, :]) * y


def _mix(x, ya, yb, yc, mg, merge_b, mod, w_rg, w_sb, w_fox, w_o, tm):
    bsz, slen, d = x.shape
    row = lambda n: pl.BlockSpec((None, tm, n), lambda b, i: (b, i, 0))
    return pl.pallas_call(
        _mix_kernel,
        out_shape=jax.ShapeDtypeStruct(x.shape, F32),
        grid=(bsz, slen // tm),
        in_specs=[row(d), row(ya.shape[2]), row(yb.shape[2]), row(yc.shape[2]), row(mg.shape[2]),
                  pl.BlockSpec((1, mg.shape[2]), lambda b, i: (0, 0)), pl.BlockSpec((None, 9, d), lambda b, i: (b, 0, 0)),
                  _resident(w_rg.shape), _resident(w_sb.shape), _resident(w_fox.shape), _resident(w_o.shape)],
        out_specs=row(d),
        compiler_params=_params(2),
        name="mix",
    )(x, ya, yb, yc, mg, merge_b, mod, w_rg, w_sb, w_fox, w_o)


def _final_kernel(x_ref, g_ref, fm_ref, o_ref):
    o_ref[...] = _norm_mod(x_ref[...], g_ref[...], fm_ref[0:1, :], fm_ref[1:2, :])


def _final(x, gain, fm, tm):
    bsz, slen, d = x.shape
    tile = pl.BlockSpec((None, tm, d), lambda b, i: (b, i, 0))
    return pl.pallas_call(
        _final_kernel,
        out_shape=jax.ShapeDtypeStruct(x.shape, F32),
        grid=(bsz, slen // tm),
        in_specs=[tile, pl.BlockSpec((1, d), lambda b, i: (0, 0)), pl.BlockSpec((None, 2, d), lambda b, i: (b, 0, 0))],
        out_specs=tile,
        compiler_params=_params(2),
        name="final_norm",
    )(x, gain, fm)


def _ffn_weights(w1, w3, w2):
    d, ff = w1.shape
    nc = ff // FF_CHUNK
    w13 = jnp.stack([w1.reshape(d, nc, FF_CHUNK), w3.reshape(d, nc, FF_CHUNK)], axis=2)
    w13 = w13.transpose(1, 0, 2, 3).reshape(nc, d, 2 * FF_CHUNK)
    return w13.astype(BF16), w2.reshape(nc, FF_CHUNK, d).astype(BF16)


def _block_diag(w):
    n, bd, _ = w.shape
    per = RG_GROUP // bd
    eye = jnp.eye(per, dtype=w.dtype)
    wg = w.reshape(n // per, per, bd, bd)
    return jnp.einsum("gpde,pq->gpdqe", wg, eye).reshape(n // per, per * bd, per * bd).astype(BF16)


def kernel(x, c, ffn1_norm, ffn1_w1, ffn1_w3, ffn1_w2, mix_norm, w_in, conv_w, conv_b, rg_wa, rg_ba, rg_wx, rg_bx, rg_lam, fox_bf, merge_b, w_rg, w_sb, w_fox, w_o, ffn2_norm, ffn2_w1, ffn2_w3, ffn2_w2, ada_w, ada_b, final_norm, final_ada_w, final_ada_b):
    bsz, slen, d = x.shape
    depth = ada_w.shape[0]
    tm = min(512, slen)
    tc = 512
    mods = _ada(c, ada_w, ada_b).reshape(depth, bsz, 9, d)
    fm = _ada(c, final_ada_w[None], final_ada_b[None]).reshape(bsz, 2, d)
    cuts = np.cumsum((RG_WIDTH, RG_WIDTH, 3 * ATT_WIDTH, 3 * ATT_WIDTH, N_HEADS)).tolist()
    row = lambda v: v.reshape(1, -1)
    for l in range(depth):
        mod = mods[l]
        x = _ffn(x, row(ffn1_norm[l]), mod, *_ffn_weights(ffn1_w1[l], ffn1_w3[l], ffn1_w2[l]), 0, tm)
        wl = w_in[l].astype(BF16)
        w_f = jnp.pad(wl[:, cuts[3] : cuts[4]], ((0, 0), (0, LANES - N_HEADS)))
        ws = (wl[:, : cuts[0]], wl[:, cuts[0] : cuts[1]], wl[:, cuts[1] : cuts[2]], wl[:, cuts[2] : cuts[3]], w_f, wl[:, cuts[4] :])
        rgx, rgg, sb_qkv, fox_qkv, f, mg = _inproj(x, row(mix_norm[l]), mod, ws, tm)
        cum_col, cum_row = _fcum(f, jnp.pad(row(fox_bf[l]), ((0, 0), (0, LANES - N_HEADS))))
        cum_row = cum_row[:, :N_HEADS].reshape(bsz, N_HEADS, slen // ATT_BLOCK, 1, ATT_BLOCK)
        ya = _rglru(rgx, rgg, conv_w[l], row(conv_b[l]), _block_diag(rg_wa[l]), row(rg_ba[l]), _block_diag(rg_wx[l]),
                    row(rg_bx[l]), row(rg_lam[l]), tc)
        yb = _sb_attention(sb_qkv)
        yc = _fox_attention(fox_qkv, cum_col, cum_row)
        x = _mix(x, ya, yb, yc, mg, row(merge_b[l]), mod, w_rg[l].astype(BF16), w_sb[l].astype(BF16),
                 w_fox[l].astype(BF16), w_o[l].astype(BF16), tm)
        x = _ffn(x, row(ffn2_norm[l]), mod, *_ffn_weights(ffn2_w1[l], ffn2_w3[l], ffn2_w2[l]), 2, tm)
    return _final(x, row(final_norm), fm, tm)
```

```python
import functools

import jax
import jax.numpy as jnp
import numpy as np
from jax import lax
from jax.experimental import pallas as pl
from jax.experimental.pallas import tpu as pltpu

F32 = jnp.float32
BF16 = jnp.bfloat16

D_MODEL = 1024
RG_WIDTH = 1024
RG_BLOCK_DIM = 64
CONV_WIDTH = 4
RG_C = 8.0
HEAD_DIM = 64
ATT_WIDTH = 512
N_HEADS = 8
D_FF = 2816
EPS = 1e-6

LANES = 128
SUBLANES = 8
MXU_DIM = 256
FF_CHUNK = MXU_DIM
ATT_BLOCK = 128
ATT_Q_ROWS = 1024
ATT_KEY_GROUP = 2
SB_SKEW = 4
FOX_SKEW = 8
LOG2E = 1.4426950408889634
RG_GROUP = MXU_DIM
VMEM_LIMIT = 56 * 1024 * 1024


def _params(n_grid):
    return pltpu.CompilerParams(dimension_semantics=("parallel",) * n_grid, vmem_limit_bytes=VMEM_LIMIT)


def _resident(shape):
    nd = len(shape)
    return pl.BlockSpec(shape, lambda *_: (0,) * nd, pipeline_mode=pl.Buffered(1))


def _softplus(y):
    return jnp.maximum(y, 0.0) + jnp.log1p(jnp.exp(-jnp.abs(y)))


def _softplus_sum(y):
    return jnp.maximum(y, 0.0) + jnp.log(1.0 + jnp.exp2(jnp.abs(y) * -LOG2E))


def _norm_mod(x, gain, shift, scale):
    y = x * lax.rsqrt(jnp.mean(x * x, axis=-1, keepdims=True) + EPS)
    return (y * gain) * (1.0 + scale) + shift


def _ada_kernel(c_ref, w_ref, b_ref, o_ref):
    c = c_ref[...]
    o_ref[...] = jnp.dot(c * jax.nn.sigmoid(c), w_ref[...], preferred_element_type=F32) + b_ref[...]


def _ada(c, w, b):
    n_l, d, n = w.shape
    bsz = c.shape[0]
    tn = 1024
    return pl.pallas_call(
        _ada_kernel,
        out_shape=jax.ShapeDtypeStruct((n_l, bsz, n), F32),
        grid=(n_l, n // tn),
        in_specs=[
            pl.BlockSpec((bsz, d), lambda l, j: (0, 0)),
            pl.BlockSpec((None, d, tn), lambda l, j: (l, 0, j)),
            pl.BlockSpec((None, 1, tn), lambda l, j: (l, 0, j)),
        ],
        out_specs=pl.BlockSpec((None, bsz, tn), lambda l, j: (l, 0, j)),
        compiler_params=_params(2),
        name="ada",
    )(c, w, b.reshape(n_l, 1, n))


def _ffn_kernel(sub, x_ref, g_ref, mod_ref, w1_ref, w3_ref, w2_ref, o_ref, h_ref, act_ref):
    x = x_ref[...]
    h = _norm_mod(x, g_ref[...], mod_ref[3 * sub : 3 * sub + 1, :], mod_ref[3 * sub + 1 : 3 * sub + 2, :])
    h_ref[...] = h.astype(BF16)
    for c0 in range(0, w1_ref.shape[1], FF_CHUNK):
        cols = slice(c0, c0 + FF_CHUNK)
        a = jnp.dot(h_ref[...], w1_ref[:, cols], preferred_element_type=F32)
        b = jnp.dot(h_ref[...], w3_ref[:, cols], preferred_element_type=F32)
        act_ref[:, cols] = ((a * jax.nn.sigmoid(a)) * b).astype(BF16)
    y = jnp.dot(act_ref[...], w2_ref[...], preferred_element_type=F32)
    gate = mod_ref[3 * sub + 2 : 3 * sub + 3, :]
    o_ref[...] = x + (0.5 * (1.0 + gate)) * y


def _ffn(x, gain, mod, w1, w3, w2, sub, tm):
    bsz, slen, d = x.shape
    tile = pl.BlockSpec((None, tm, d), lambda b, i: (b, i, 0))
    return pl.pallas_call(
        functools.partial(_ffn_kernel, sub),
        out_shape=jax.ShapeDtypeStruct(x.shape, F32),
        grid=(bsz, slen // tm),
        in_specs=[
            tile,
            pl.BlockSpec((1, d), lambda b, i: (0, 0)),
            pl.BlockSpec((None, 9, d), lambda b, i: (b, 0, 0)),
            _resident(w1.shape),
            _resident(w3.shape),
            _resident(w2.shape),
        ],
        out_specs=tile,
        scratch_shapes=[pltpu.VMEM((tm, d), BF16), pltpu.VMEM((tm, w2.shape[0]), BF16)],
        compiler_params=_params(2),
        name="ffn",
    )(x, gain, mod, w1, w3, w2)


def _inproj_kernel(x_ref, g_ref, mod_ref, wx_ref, wg_ref, wsb_ref, wfox_ref, wf_ref, wmg_ref,
                   rgx_ref, rgg_ref, sb_ref, fox_ref, f_ref, mg_ref, h_ref):
    h = _norm_mod(x_ref[...], g_ref[...], mod_ref[3:4, :], mod_ref[4:5, :])
    h_ref[...] = h.astype(BF16)
    for w_ref, o_ref in ((wx_ref, rgx_ref), (wg_ref, rgg_ref), (wsb_ref, sb_ref), (wfox_ref, fox_ref),
                         (wf_ref, f_ref), (wmg_ref, mg_ref)):
        n = w_ref.shape[1]
        step = min(n, 2 * MXU_DIM)
        for c0 in range(0, n, step):
            o_ref[:, c0 : c0 + step] = jnp.dot(
                h_ref[...], w_ref[:, c0 : c0 + step], preferred_element_type=F32).astype(o_ref.dtype)


def _inproj(x, gain, mod, ws, tm):
    bsz, slen, d = x.shape
    out_dtypes = (F32, F32, BF16, BF16, F32, BF16)
    row = lambda n: pl.BlockSpec((None, tm, n), lambda b, i: (b, i, 0))
    return pl.pallas_call(
        _inproj_kernel,
        out_shape=[jax.ShapeDtypeStruct((bsz, slen, w.shape[1]), dt) for w, dt in zip(ws, out_dtypes)],
        grid=(bsz, slen // tm),
        in_specs=[row(d), pl.BlockSpec((1, d), lambda b, i: (0, 0)), pl.BlockSpec((None, 9, d), lambda b, i: (b, 0, 0))]
        + [_resident(w.shape) for w in ws],
        out_specs=[row(w.shape[1]) for w in ws],
        scratch_shapes=[pltpu.VMEM((tm, d), BF16)],
        compiler_params=_params(2),
        name="inproj",
    )(x, gain, mod, *ws)


def _fcum_kernel(f_ref, bf_ref, col_ref, row_ref):
    slen = f_ref.shape[0]
    c = -_softplus(-(f_ref[...] + bf_ref[...]))
    t = lax.broadcasted_iota(jnp.int32, c.shape, 0)
    d = 1
    while d < slen:
        c = c + jnp.where(t >= d, pltpu.roll(c, d, 0), 0.0)
        d *= 2
    c = c * LOG2E
    col_ref[...] = c
    row_ref[...] = c.T


def _fcum(f, bf):
    bsz, slen, n = f.shape
    return pl.pallas_call(
        _fcum_kernel,
        out_shape=[jax.ShapeDtypeStruct((bsz, slen, n), F32), jax.ShapeDtypeStruct((bsz, n, slen), F32)],
        grid=(bsz,),
        in_specs=[pl.BlockSpec((None, slen, n), lambda b: (b, 0, 0)), pl.BlockSpec((1, n), lambda b: (0, 0))],
        out_specs=[pl.BlockSpec((None, slen, n), lambda b: (b, 0, 0)), pl.BlockSpec((None, n, slen), lambda b: (b, 0, 0))],
        compiler_params=_params(1),
        name="fcum",
    )(f, bf)


def _rglru_kernel(x_ref, gt_ref, cw_ref, cb_ref, wa_ref, ba_ref, wx_ref, bx_ref, lam_ref, o_ref,
                  xpad_ref, a_ref, u_ref):
    slen, tc = a_ref.shape
    pad = SUBLANES
    xpad_ref[0:pad, :] = jnp.zeros((pad, tc), F32)
    xpad_ref[pad : pad + slen, :] = x_ref[...]
    sp_lam = _softplus(-lam_ref[...])
    rc = min(256, slen)
    for r0 in range(0, slen, rc):
        xa = cb_ref[...]
        for k in range(CONV_WIDTH):
            xa = xa + cw_ref[k : k + 1, :] * xpad_ref[pl.ds(pad + r0 - (CONV_WIDTH - 1) + k, rc), :]
        xab = xa.astype(BF16)
        for g in range(tc // RG_GROUP):
            cs = slice(g * RG_GROUP, (g + 1) * RG_GROUP)
            r = jax.nn.sigmoid(jnp.dot(xab[:, cs], wa_ref[g], preferred_element_type=F32) + ba_ref[:, cs])
            i = jax.nn.sigmoid(jnp.dot(xab[:, cs], wx_ref[g], preferred_element_type=F32) + bx_ref[:, cs])
            log_a = (-RG_C * r) * sp_lam[:, cs]
            a = jnp.exp(log_a)
            u = jnp.sqrt(-jnp.tanh(log_a) * (a * a + 1.0)) * (i * xa[:, cs])
            a_ref[r0 : r0 + rc, cs] = a
            u_ref[r0 : r0 + rc, cs] = u

    row = lax.broadcasted_iota(jnp.int32, (SUBLANES, tc), 0)

    def scan8(k, carry):
        o = pl.multiple_of(k * SUBLANES, SUBLANES)
        a = a_ref[pl.ds(o, SUBLANES), :]
        u = u_ref[pl.ds(o, SUBLANES), :]
        for d in (1, 2, 4):
            a_s = jnp.where(row >= d, pltpu.roll(a, d, 0), 1.0)
            u_s = jnp.where(row >= d, pltpu.roll(u, d, 0), 0.0)
            u = u + a * u_s
            a = a * a_s
        h = a * carry + u
        u_ref[pl.ds(o, SUBLANES), :] = h
        return jnp.broadcast_to(h[SUBLANES - 1 :, :], (SUBLANES, tc))

    lax.fori_loop(0, slen // SUBLANES, scan8, jnp.zeros((SUBLANES, tc), F32), unroll=8)
    for r0 in range(0, slen, rc):
        gate = jax.nn.gelu(gt_ref[r0 : r0 + rc, :], approximate=True)
        o_ref[r0 : r0 + rc, :] = (gate * u_ref[r0 : r0 + rc, :]).astype(o_ref.dtype)


def _rglru(rgx, rgg, conv_w, conv_b, wa_bd, ba, wx_bd, bx, lam, tc):
    bsz, slen, width = rgx.shape
    tile = pl.BlockSpec((None, slen, tc), lambda b, j: (b, 0, j))
    vec = pl.BlockSpec((1, tc), lambda b, j: (0, j))
    wblk = pl.BlockSpec((tc // RG_GROUP, RG_GROUP, RG_GROUP), lambda b, j: (j, 0, 0))
    return pl.pallas_call(
        _rglru_kernel,
        out_shape=jax.ShapeDtypeStruct(rgx.shape, BF16),
        grid=(bsz, width // tc),
        in_specs=[tile, tile, pl.BlockSpec((CONV_WIDTH, tc), lambda b, j: (0, j)), vec, wblk, vec, wblk, vec, vec],
        out_specs=tile,
        scratch_shapes=[pltpu.VMEM((slen + SUBLANES, tc), F32), pltpu.VMEM((slen, tc), F32), pltpu.VMEM((slen, tc), F32)],
        compiler_params=_params(2),
        name="rglru",
    )(rgx, rgg, conv_w, conv_b, wa_bd, ba, wx_bd, bx, lam)


def _qk(qm, k):
    return lax.dot_general(qm, k, (((1,), (1,)), ((), ())), preferred_element_type=F32)


def _masked_q(q_ref, qm_ref, q0, tq, scale):
    lm0 = lax.broadcasted_iota(jnp.int32, (tq, LANES), 1) < HEAD_DIM
    q = (q_ref[pl.ds(q0, tq), :].astype(F32) * scale).astype(BF16)
    qm_ref[0] = jnp.where(lm0, q, jnp.zeros_like(q))
    qm_ref[1] = jnp.where(lm0, jnp.zeros_like(q), q)
    return lm0


def _band_units(n_chunks):
    units = []
    for h in range(2):
        for r in range(n_chunks):
            kbs = [(d, d == r) for d in range(r, -1, -1)]
            units += [(h, r, kbs[i : i + ATT_KEY_GROUP]) for i in range(0, len(kbs), ATT_KEY_GROUP)]
    return units


def _full_units(n_chunks):
    kbs = [(-i, False) for i in range(ATT_KEY_GROUP)]
    return [(h, r, kbs) for h in range(2) for r in range(n_chunks)]


def _skewed(stages, n, skew):
    for slot in range(n + (len(stages) - 1) * skew):
        for s, stage in enumerate(stages):
            if 0 <= slot - s * skew < n:
                stage(slot - s * skew)


def _key_rows(ref, base, off, n=1):
    tk = ATT_BLOCK
    return ref[pl.ds(pl.multiple_of((base + off) * tk, tk), n * tk), :]


def _sb_kernel(tq, q_ref, k_ref, v_ref, tri_ref, o_ref, qm_ref, c_ref, acc_ref):
    tk = ATT_BLOCK
    n_chunks = tq // tk
    strict = lax.broadcasted_iota(jnp.int32, (tk, tk), 1) < lax.broadcasted_iota(jnp.int32, (tk, tk), 0)

    def run(units, base):
        zs = [None] * len(units)
        mids = [None] * len(units)

        def scores(i):
            h, r, kbs = units[i]
            qm = qm_ref[h, r * tk : (r + 1) * tk, :]
            zs[i] = [_qk(qm, _key_rows(k_ref, base, off)) for off, _ in kbs]

        def log_keep(i):
            ml = []
            for (_, diag), z in zip(units[i][2], zs[i]):
                sp = _softplus_sum(z)
                nk = jnp.where(strict, sp, 0.0) if diag else sp
                ml.append((z - sp, jnp.dot(nk.astype(BF16), tri_ref[...], preferred_element_type=F32),
                           jnp.sum(nk, axis=1, keepdims=True)))
            zs[i] = None
            mids[i] = ml

        def weigh(i):
            h, r, kbs = units[i]
            rows = slice(r * tk, (r + 1) * tk)
            c = c_ref[h, rows, :]
            ws = []
            for (_, diag), (lw, cs, rs) in zip(kbs, mids[i]):
                w = jnp.exp(lw - cs - c)
                ws.append((jnp.where(strict, w, 0.0) if diag else w).astype(BF16))
                c = c + rs
            mids[i] = None
            c_ref[h, rows, :] = c
            wcat = ws[0] if len(ws) == 1 else jnp.concatenate(ws[::-1], axis=1)
            acc_ref[h, rows, :] += jnp.dot(wcat, _key_rows(v_ref, base, kbs[-1][0], len(kbs)), preferred_element_type=F32)

        _skewed((scores, log_keep, weigh), len(units), SB_SKEW)

    def q_block(qi, carry):
        q0 = pl.multiple_of(qi * tq, tq)
        lm0 = _masked_q(q_ref, qm_ref, q0, tq, HEAD_DIM ** -0.5)
        c_ref[...] = jnp.zeros_like(c_ref)
        acc_ref[...] = jnp.zeros_like(acc_ref)
        run(_band_units(n_chunks), qi * n_chunks)

        def k_group(j, carry):
            run(_full_units(n_chunks), qi * n_chunks - 1 - j * ATT_KEY_GROUP)
            return carry

        lax.fori_loop(0, qi * (n_chunks // ATT_KEY_GROUP), k_group, 0)
        o_ref[pl.ds(q0, tq), :] = jnp.where(lm0, acc_ref[0], acc_ref[1]).astype(o_ref.dtype)
        return carry

    lax.fori_loop(0, q_ref.shape[0] // tq, q_block, 0)


def _tri():
    return jnp.asarray(np.tril(np.ones((ATT_BLOCK, ATT_BLOCK), np.float32), -1), BF16)


def _head_pair_specs(slen, n_pairs):
    return [pl.BlockSpec((None, slen, LANES), lambda b, p, o=o: (b, 0, o * n_pairs + p)) for o in range(3)]


def _sb_attention(qkv):
    bsz, slen, _ = qkv.shape
    n_pairs = ATT_WIDTH // LANES
    tq = min(ATT_Q_ROWS, slen)
    return pl.pallas_call(
        functools.partial(_sb_kernel, tq),
        out_shape=jax.ShapeDtypeStruct((bsz, slen, ATT_WIDTH), BF16),
        grid=(bsz, n_pairs),
        in_specs=_head_pair_specs(slen, n_pairs) + [pl.BlockSpec((ATT_BLOCK, ATT_BLOCK), lambda b, p: (0, 0))],
        out_specs=pl.BlockSpec((None, slen, LANES), lambda b, p: (b, 0, p)),
        scratch_shapes=[pltpu.VMEM((2, tq, LANES), BF16), pltpu.VMEM((2, tq, LANES), F32), pltpu.VMEM((2, tq, LANES), F32)],
        compiler_params=_params(2),
        name="sb_attn",
    )(qkv, qkv, qkv, _tri())


def _fox_kernel(tq, q_ref, k_ref, v_ref, col_ref, row_ref, o_ref, qm_ref, ct_ref, m_ref, acc_ref):
    tk = ATT_BLOCK
    n_chunks = tq // tk
    pair = pl.program_id(1)
    causal = lax.broadcasted_iota(jnp.int32, (tk, tk), 1) <= lax.broadcasted_iota(jnp.int32, (tk, tk), 0)

    def run(units, base):
        zs = [None] * len(units)
        vms = {}

        def scores(i):
            h, r, kbs = units[i]
            qm = qm_ref[h, r * tk : (r + 1) * tk, :]
            zs[i] = [_qk(qm, _key_rows(k_ref, base, off)) for off, _ in kbs]

        def update(i):
            h, r, kbs = units[i]
            rows = slice(r * tk, (r + 1) * tk)
            ss = []
            for (off, diag), z in zip(kbs, zs[i]):
                s = z - row_ref[2 * pair + h, base + off]
                ss.append(jnp.where(causal, s, -jnp.inf) if diag else s)
            zs[i] = None
            top = functools.reduce(jnp.maximum, ss)
            ct = ct_ref[h, rows, :]
            m = m_ref[h, rows, :]
            m_new = jnp.maximum(m, jnp.max(top, axis=1, keepdims=True) + ct)
            shift = ct - m_new
            ps = [jnp.exp2(s + shift).astype(BF16) for s in ss]
            pcat = ps[0] if len(ps) == 1 else jnp.concatenate(ps[::-1], axis=1)
            key = (h, kbs[-1][0], len(kbs))
            if key not in vms:
                v = _key_rows(v_ref, base, kbs[-1][0], len(kbs))
                lane = lax.broadcasted_iota(jnp.int32, v.shape, 1)
                vms[key] = jnp.where(lane < HEAD_DIM if h == 0 else lane >= HEAD_DIM, v, jnp.ones_like(v))
            acc_ref[h, rows, :] = jnp.exp2(m - m_new) * acc_ref[h, rows, :] + jnp.dot(
                pcat, vms[key], preferred_element_type=F32)
            m_ref[h, rows, :] = m_new

        _skewed((scores, update), len(units), FOX_SKEW)

    def q_block(qi, carry):
        q0 = pl.multiple_of(qi * tq, tq)
        lm0 = _masked_q(q_ref, qm_ref, q0, tq, LOG2E * HEAD_DIM ** -0.5)
        col = col_ref[pl.ds(q0, tq), :]
        lane = lax.broadcasted_iota(jnp.int32, (tq, LANES), 1)
        for h in range(2):
            ct = jnp.sum(jnp.where(lane == 2 * pair + h, col, 0.0), axis=1, keepdims=True)
            ct_ref[h] = jnp.broadcast_to(ct, (tq, LANES))
        m_ref[...] = jnp.full(m_ref.shape, -jnp.inf, F32)
        acc_ref[...] = jnp.zeros_like(acc_ref)
        run(_band_units(n_chunks), qi * n_chunks)

        def k_group(j, carry):
            run(_full_units(n_chunks), qi * n_chunks - 1 - j * ATT_KEY_GROUP)
            return carry

        lax.fori_loop(0, qi * (n_chunks // ATT_KEY_GROUP), k_group, 0)
        acca = acc_ref[0]
        accb = acc_ref[1]
        out = jnp.where(lm0, acca / pltpu.roll(acca, HEAD_DIM, 1), accb / pltpu.roll(accb, HEAD_DIM, 1))
        o_ref[pl.ds(q0, tq), :] = out.astype(o_ref.dtype)
        return carry

    lax.fori_loop(0, q_ref.shape[0] // tq, q_block, 0)


def _fox_attention(qkv, cum_col, cum_row):
    bsz, slen, _ = qkv.shape
    n_pairs = ATT_WIDTH // LANES
    tq = min(ATT_Q_ROWS, slen)
    state = pltpu.VMEM((2, tq, LANES), F32)
    return pl.pallas_call(
        functools.partial(_fox_kernel, tq),
        out_shape=jax.ShapeDtypeStruct((bsz, slen, ATT_WIDTH), BF16),
        grid=(bsz, n_pairs),
        in_specs=_head_pair_specs(slen, n_pairs) + [
            pl.BlockSpec((None, slen, LANES), lambda b, p: (b, 0, 0)),
            pl.BlockSpec((None,) + cum_row.shape[1:], lambda b, p: (b, 0, 0, 0, 0)),
        ],
        out_specs=pl.BlockSpec((None, slen, LANES), lambda b, p: (b, 0, p)),
        scratch_shapes=[pltpu.VMEM((2, tq, LANES), BF16), state, state, state],
        compiler_params=_params(2),
        name="fox_attn",
    )(qkv, qkv, qkv, cum_col, cum_row)


def _mix_kernel(x_ref, ya_ref, yb_ref, yc_ref, mg_ref, mb_ref, mod_ref, wrg_ref, wsb_ref, wfox_ref, wo_ref, o_ref):
    d = x_ref.shape[1]
    mixed = None
    for n, (y_ref, w_ref) in enumerate(((ya_ref, wrg_ref), (yb_ref, wsb_ref), (yc_ref, wfox_ref))):
        cs = slice(n * d, (n + 1) * d)
        g = jax.nn.sigmoid(mg_ref[:, cs].astype(F32) + mb_ref[:, cs])
        t = g * jnp.dot(y_ref[...], w_ref[...], preferred_element_type=F32)
        mixed = t if mixed is None else mixed + t
    y = jnp.dot(mixed.astype(BF16), wo_ref[...], preferred_element_type=F32)
    o_ref[...] = x_ref[...] + (1.0 + mod_ref[5:6, :]) * y


def _mix(x, ya, yb, yc, mg, merge_b, mod, w_rg, w_sb, w_fox, w_o, tm):
    bsz, slen, d = x.shape
    row = lambda n: pl.BlockSpec((None, tm, n), lambda b, i: (b, i, 0))
    return pl.pallas_call(
        _mix_kernel,
        out_shape=jax.ShapeDtypeStruct(x.shape, F32),
        grid=(bsz, slen // tm),
        in_specs=[row(d), row(ya.shape[2]), row(yb.shape[2]), row(yc.shape[2]), row(mg.shape[2]),
                  pl.BlockSpec((1, mg.shape[2]), lambda b, i: (0, 0)), pl.BlockSpec((None, 9, d), lambda b, i: (b, 0, 0)),
                  _resident(w_rg.shape), _resident(w_sb.shape), _resident(w_fox.shape), _resident(w_o.shape)],
        out_specs=row(d),
        compiler_params=_params(2),
        name="mix",
    )(x, ya, yb, yc, mg, merge_b, mod, w_rg, w_sb, w_fox, w_o)


def _final_kernel(x_ref, g_ref, fm_ref, o_ref):
    o_ref[...] = _norm_mod(x_ref[...], g_ref[...], fm_ref[0:1, :], fm_ref[1:2, :])


def _final(x, gain, fm, tm):
    bsz, slen, d = x.shape
    tile = pl.BlockSpec((None, tm, d), lambda b, i: (b, i, 0))
    return pl.pallas_call(
        _final_kernel,
        out_shape=jax.ShapeDtypeStruct(x.shape, F32),
        grid=(bsz, slen // tm),
        in_specs=[tile, pl.BlockSpec((1, d), lambda b, i: (0, 0)), pl.BlockSpec((None, 2, d), lambda b, i: (b, 0, 0))],
        out_specs=tile,
        compiler_params=_params(2),
        name="final_norm",
    )(x, gain, fm)


def _block_diag(w):
    n, bd, _ = w.shape
    per = RG_GROUP // bd
    eye = jnp.eye(per, dtype=w.dtype)
    wg = w.reshape(n // per, per, bd, bd)
    return jnp.einsum("gpde,pq->gpdqe", wg, eye).reshape(n // per, per * bd, per * bd).astype(BF16)


def kernel(x, c, ffn1_norm, ffn1_w1, ffn1_w3, ffn1_w2, mix_norm, w_in, conv_w, conv_b, rg_wa, rg_ba, rg_wx, rg_bx, rg_lam, fox_bf, merge_b, w_rg, w_sb, w_fox, w_o, ffn2_norm, ffn2_w1, ffn2_w3, ffn2_w2, ada_w, ada_b, final_norm, final_ada_w, final_ada_b):
    bsz, slen, d = x.shape
    depth = ada_w.shape[0]
    tm = min(512, slen)
    tc = 512
    mods = _ada(c, ada_w, ada_b).reshape(depth, bsz, 9, d)
    fm = _ada(c, final_ada_w[None], final_ada_b[None]).reshape(bsz, 2, d)
    cuts = np.cumsum((RG_WIDTH, RG_WIDTH, 3 * ATT_WIDTH, 3 * ATT_WIDTH, N_HEADS)).tolist()
    row = lambda v: v.reshape(1, -1)
    for l in range(depth):
        mod = mods[l]
        x = _ffn(x, row(ffn1_norm[l]), mod, ffn1_w1[l].astype(BF16), ffn1_w3[l].astype(BF16), ffn1_w2[l].astype(BF16), 0, tm)
        wl = w_in[l].astype(BF16)
        w_f = jnp.pad(wl[:, cuts[3] : cuts[4]], ((0, 0), (0, LANES - N_HEADS)))
        ws = (wl[:, : cuts[0]], wl[:, cuts[0] : cuts[1]], wl[:, cuts[1] : cuts[2]], wl[:, cuts[2] : cuts[3]], w_f, wl[:, cuts[4] :])
        rgx, rgg, sb_qkv, fox_qkv, f, mg = _inproj(x, row(mix_norm[l]), mod, ws, tm)
        cum_col, cum_row = _fcum(f, jnp.pad(row(fox_bf[l]), ((0, 0), (0, LANES - N_HEADS))))
        cum_row = cum_row[:, :N_HEADS].reshape(bsz, N_HEADS, slen // ATT_BLOCK, 1, ATT_BLOCK)
        ya = _rglru(rgx, rgg, conv_w[l], row(conv_b[l]), _block_diag(rg_wa[l]), row(rg_ba[l]), _block_diag(rg_wx[l]),
                    row(rg_bx[l]), row(rg_lam[l]), tc)
        yb = _sb_attention(sb_qkv)
        yc = _fox_attention(fox_qkv, cum_col, cum_row)
        x = _mix(x, ya, yb, yc, mg, row(merge_b[l]), mod, w_rg[l].astype(BF16), w_sb[l].astype(BF16),
                 w_fox[l].astype(BF16), w_o[l].astype(BF16), tm)
        x = _ffn(x, row(ffn2_norm[l]), mod, ffn2_w1[l].astype(BF16), ffn2_w3[l].astype(BF16), ffn2_w2[l].astype(BF16), 2, tm)
    return _final(x, row(final_norm), fm, tm)
```

```python
import functools

import jax
import jax.numpy as jnp
import numpy as np
from jax import lax
from jax.experimental import pallas as pl
from jax.experimental.pallas import tpu as pltpu

F32 = jnp.float32
BF16 = jnp.bfloat16

D_MODEL = 1024
RG_WIDTH = 1024
RG_BLOCK_DIM = 64
CONV_WIDTH = 4
RG_C = 8.0
HEAD_DIM = 64
ATT_WIDTH = 512
N_HEADS = 8
D_FF = 2816
EPS = 1e-6

LANES = 128
SUBLANES = 8
MXU_DIM = 256
FF_CHUNK = MXU_DIM
ATT_BLOCK = 128
ATT_Q_ROWS = 1024
SB_KEY_GROUP = 2
FOX_KEY_GROUP = 4
SB_SKEW = 3
FOX_SKEW = 4
LOG2E = 1.4426950408889634
RG_GROUP = MXU_DIM
VMEM_LIMIT = 56 * 1024 * 1024


def _params(n_grid):
    return pltpu.CompilerParams(dimension_semantics=("parallel",) * n_grid, vmem_limit_bytes=VMEM_LIMIT)


def _resident(shape):
    nd = len(shape)
    return pl.BlockSpec(shape, lambda *_: (0,) * nd, pipeline_mode=pl.Buffered(1))


def _softplus(y):
    return jnp.maximum(y, 0.0) + jnp.log1p(jnp.exp(-jnp.abs(y)))


def _softplus_sum(y):
    return jnp.maximum(y, 0.0) + jnp.log(1.0 + jnp.exp2(jnp.abs(y) * -LOG2E))


def _norm_mod(x, gain, shift, scale):
    y = x * lax.rsqrt(jnp.mean(x * x, axis=-1, keepdims=True) + EPS)
    return (y * gain) * (1.0 + scale) + shift


def _ada_kernel(c_ref, w_ref, b_ref, o_ref):
    c = c_ref[...]
    o_ref[...] = jnp.dot(c * jax.nn.sigmoid(c), w_ref[...], preferred_element_type=F32) + b_ref[...]


def _ada(c, w, b):
    n_l, d, n = w.shape
    bsz = c.shape[0]
    tn = 1024
    return pl.pallas_call(
        _ada_kernel,
        out_shape=jax.ShapeDtypeStruct((n_l, bsz, n), F32),
        grid=(n_l, n // tn),
        in_specs=[
            pl.BlockSpec((bsz, d), lambda l, j: (0, 0)),
            pl.BlockSpec((None, d, tn), lambda l, j: (l, 0, j)),
            pl.BlockSpec((None, 1, tn), lambda l, j: (l, 0, j)),
        ],
        out_specs=pl.BlockSpec((None, bsz, tn), lambda l, j: (l, 0, j)),
        compiler_params=_params(2),
        name="ada",
    )(c, w, b.reshape(n_l, 1, n))


def _ffn_kernel(sub, last, x_ref, g_ref, mod_ref, w1_ref, w3_ref, w2_ref, fg_ref, fm_ref, o_ref, h_ref, act_ref):
    x = x_ref[...]
    h = _norm_mod(x, g_ref[...], mod_ref[3 * sub : 3 * sub + 1, :], mod_ref[3 * sub + 1 : 3 * sub + 2, :])
    h_ref[...] = h.astype(BF16)
    for c0 in range(0, w1_ref.shape[1], FF_CHUNK):
        cols = slice(c0, c0 + FF_CHUNK)
        a = jnp.dot(h_ref[...], w1_ref[:, cols], preferred_element_type=F32)
        b = jnp.dot(h_ref[...], w3_ref[:, cols], preferred_element_type=F32)
        act_ref[:, cols] = ((a * jax.nn.sigmoid(a)) * b).astype(BF16)
    y = jnp.dot(act_ref[...], w2_ref[...], preferred_element_type=F32)
    gate = mod_ref[3 * sub + 2 : 3 * sub + 3, :]
    out = x + (0.5 * (1.0 + gate)) * y
    if last:
        out = _norm_mod(out, fg_ref[...], fm_ref[0:1, :], fm_ref[1:2, :])
    o_ref[...] = out


def _ffn(x, gain, mod, w1, w3, w2, final_gain, fm, sub, last, tm):
    bsz, slen, d = x.shape
    tile = pl.BlockSpec((None, tm, d), lambda b, i: (b, i, 0))
    return pl.pallas_call(
        functools.partial(_ffn_kernel, sub, last),
        out_shape=jax.ShapeDtypeStruct(x.shape, F32),
        grid=(bsz, slen // tm),
        in_specs=[
            tile,
            pl.BlockSpec((1, d), lambda b, i: (0, 0)),
            pl.BlockSpec((None, 9, d), lambda b, i: (b, 0, 0)),
            _resident(w1.shape),
            _resident(w3.shape),
            _resident(w2.shape),
            pl.BlockSpec((1, d), lambda b, i: (0, 0)),
            pl.BlockSpec((None, 2, d), lambda b, i: (b, 0, 0)),
        ],
        out_specs=tile,
        scratch_shapes=[pltpu.VMEM((tm, d), BF16), pltpu.VMEM((tm, w2.shape[0]), BF16)],
        compiler_params=_params(2),
        name="ffn",
    )(x, gain, mod, w1, w3, w2, final_gain, fm)


def _inproj_kernel(x_ref, g_ref, mod_ref, wx_ref, wg_ref, wsb_ref, wfox_ref, wf_ref, wmg_ref,
                   rgx_ref, rgg_ref, sb_ref, fox_ref, f_ref, mg_ref, h_ref):
    h = _norm_mod(x_ref[...], g_ref[...], mod_ref[3:4, :], mod_ref[4:5, :])
    h_ref[...] = h.astype(BF16)
    for w_ref, o_ref in ((wx_ref, rgx_ref), (wg_ref, rgg_ref), (wsb_ref, sb_ref), (wfox_ref, fox_ref),
                         (wf_ref, f_ref), (wmg_ref, mg_ref)):
        n = w_ref.shape[1]
        step = min(n, 2 * MXU_DIM)
        for c0 in range(0, n, step):
            o_ref[:, c0 : c0 + step] = jnp.dot(
                h_ref[...], w_ref[:, c0 : c0 + step], preferred_element_type=F32).astype(o_ref.dtype)


def _inproj(x, gain, mod, ws, tm):
    bsz, slen, d = x.shape
    out_dtypes = (F32, F32, BF16, BF16, F32, BF16)
    row = lambda n: pl.BlockSpec((None, tm, n), lambda b, i: (b, i, 0))
    return pl.pallas_call(
        _inproj_kernel,
        out_shape=[jax.ShapeDtypeStruct((bsz, slen, w.shape[1]), dt) for w, dt in zip(ws, out_dtypes)],
        grid=(bsz, slen // tm),
        in_specs=[row(d), pl.BlockSpec((1, d), lambda b, i: (0, 0)), pl.BlockSpec((None, 9, d), lambda b, i: (b, 0, 0))]
        + [_resident(w.shape) for w in ws],
        out_specs=[row(w.shape[1]) for w in ws],
        scratch_shapes=[pltpu.VMEM((tm, d), BF16)],
        compiler_params=_params(2),
        name="inproj",
    )(x, gain, mod, *ws)


def _fcum_kernel(f_ref, bf_ref, col_ref, row_ref):
    slen = f_ref.shape[0]
    c = -_softplus(-(f_ref[...] + bf_ref[...]))
    t = lax.broadcasted_iota(jnp.int32, c.shape, 0)
    d = 1
    while d < slen:
        c = c + jnp.where(t >= d, pltpu.roll(c, d, 0), 0.0)
        d *= 2
    c = c * LOG2E
    col_ref[...] = c
    row_ref[...] = c.T


def _fcum(f, bf):
    bsz, slen, n = f.shape
    return pl.pallas_call(
        _fcum_kernel,
        out_shape=[jax.ShapeDtypeStruct((bsz, slen, n), F32), jax.ShapeDtypeStruct((bsz, n, slen), F32)],
        grid=(bsz,),
        in_specs=[pl.BlockSpec((None, slen, n), lambda b: (b, 0, 0)), pl.BlockSpec((1, n), lambda b: (0, 0))],
        out_specs=[pl.BlockSpec((None, slen, n), lambda b: (b, 0, 0)), pl.BlockSpec((None, n, slen), lambda b: (b, 0, 0))],
        compiler_params=_params(1),
        name="fcum",
    )(f, bf)


def _rglru_kernel(x_ref, gt_ref, cw_ref, cb_ref, wa_ref, ba_ref, wx_ref, bx_ref, lam_ref, o_ref,
                  xpad_ref, a_ref, u_ref):
    slen, tc = a_ref.shape
    pad = SUBLANES
    xpad_ref[0:pad, :] = jnp.zeros((pad, tc), F32)
    xpad_ref[pad : pad + slen, :] = x_ref[...]
    sp_lam = _softplus(-lam_ref[...])
    rc = min(256, slen)
    for r0 in range(0, slen, rc):
        ext = xpad_ref[r0 : r0 + pad + rc, :]
        xa = cb_ref[...] + cw_ref[CONV_WIDTH - 1 :, :] * ext[pad:, :]
        for back in range(1, CONV_WIDTH):
            xa = xa + cw_ref[CONV_WIDTH - 1 - back : CONV_WIDTH - back, :] * pltpu.roll(ext, back, 0)[pad:, :]
        xab = xa.astype(BF16)
        for g in range(tc // RG_GROUP):
            cs = slice(g * RG_GROUP, (g + 1) * RG_GROUP)
            r = jax.nn.sigmoid(jnp.dot(xab[:, cs], wa_ref[g], preferred_element_type=F32) + ba_ref[:, cs])
            i = jax.nn.sigmoid(jnp.dot(xab[:, cs], wx_ref[g], preferred_element_type=F32) + bx_ref[:, cs])
            log_a = (-RG_C * r) * sp_lam[:, cs]
            a = jnp.exp(log_a)
            y = -jnp.tanh(log_a) * (a * a + 1.0)
            root = jnp.where(y > 0.0, y * lax.rsqrt(y), 0.0)
            u = root * (i * xa[:, cs])
            a_ref[r0 : r0 + rc, cs] = a
            u_ref[r0 : r0 + rc, cs] = u

    row = lax.broadcasted_iota(jnp.int32, (SUBLANES, tc), 0)

    def scan8(k, carry):
        o = pl.multiple_of(k * SUBLANES, SUBLANES)
        a = a_ref[pl.ds(o, SUBLANES), :]
        u = u_ref[pl.ds(o, SUBLANES), :]
        for d in (1, 2, 4):
            a_s = jnp.where(row >= d, pltpu.roll(a, d, 0), 1.0)
            u_s = jnp.where(row >= d, pltpu.roll(u, d, 0), 0.0)
            u = u + a * u_s
            a = a * a_s
        h = a * carry + u
        u_ref[pl.ds(o, SUBLANES), :] = h
        return jnp.broadcast_to(h[SUBLANES - 1 :, :], (SUBLANES, tc))

    lax.fori_loop(0, slen // SUBLANES, scan8, jnp.zeros((SUBLANES, tc), F32), unroll=8)
    for r0 in range(0, slen, rc):
        gate = jax.nn.gelu(gt_ref[r0 : r0 + rc, :], approximate=True)
        o_ref[r0 : r0 + rc, :] = (gate * u_ref[r0 : r0 + rc, :]).astype(o_ref.dtype)


def _rglru(rgx, rgg, conv_w, conv_b, wa_bd, ba, wx_bd, bx, lam, tc):
    bsz, slen, width = rgx.shape
    tile = pl.BlockSpec((None, slen, tc), lambda b, j: (b, 0, j))
    vec = pl.BlockSpec((1, tc), lambda b, j: (0, j))
    wblk = pl.BlockSpec((tc // RG_GROUP, RG_GROUP, RG_GROUP), lambda b, j: (j, 0, 0))
    return pl.pallas_call(
        _rglru_kernel,
        out_shape=jax.ShapeDtypeStruct(rgx.shape, BF16),
        grid=(bsz, width // tc),
        in_specs=[tile, tile, pl.BlockSpec((CONV_WIDTH, tc), lambda b, j: (0, j)), vec, wblk, vec, wblk, vec, vec],
        out_specs=tile,
        scratch_shapes=[pltpu.VMEM((slen + SUBLANES, tc), F32), pltpu.VMEM((slen, tc), F32), pltpu.VMEM((slen, tc), F32)],
        compiler_params=_params(2),
        name="rglru",
    )(rgx, rgg, conv_w, conv_b, wa_bd, ba, wx_bd, bx, lam)


def _qk(qm, k):
    return lax.dot_general(qm, k, (((1,), (1,)), ((), ())), preferred_element_type=F32)


def _group_scores(qm, k_ref, base, kbs, fused):
    tk = ATT_BLOCK
    if not fused:
        return [_qk(qm, _key_rows(k_ref, base, off)) for off, _ in kbs]
    z = _qk(qm, _key_rows(k_ref, base, kbs[-1][0], len(kbs)))
    return [z[:, j * tk : (j + 1) * tk] for j in reversed(range(len(kbs)))]


def _masked_q(q_ref, qm_ref, q0, tq, scale):
    lm0 = lax.broadcasted_iota(jnp.int32, (tq, LANES), 1) < HEAD_DIM
    q = (q_ref[pl.ds(q0, tq), :].astype(F32) * scale).astype(BF16)
    qm_ref[0] = jnp.where(lm0, q, jnp.zeros_like(q))
    qm_ref[1] = jnp.where(lm0, jnp.zeros_like(q), q)
    return lm0


def _band_units(n_chunks, group):
    units = []
    for h in range(2):
        for r in range(n_chunks):
            kbs = [(d, d == r) for d in range(r, -1, -1)]
            units += [(h, r, kbs[i : i + group]) for i in range(0, len(kbs), group)]
    return units


def _full_units(n_chunks, group):
    kbs = [(-i, False) for i in range(group)]
    return [(h, r, kbs) for h in range(2) for r in range(n_chunks)]


def _skewed(stages, n, skew):
    for slot in range(n + (len(stages) - 1) * skew):
        for s, stage in enumerate(stages):
            if 0 <= slot - s * skew < n:
                stage(slot - s * skew)


def _key_rows(ref, base, off, n=1):
    tk = ATT_BLOCK
    return ref[pl.ds(pl.multiple_of((base + off) * tk, tk), n * tk), :]


def _sb_kernel(tq, q_ref, k_ref, v_ref, tri_ref, o_ref, qm_ref, c_ref, acc_ref):
    tk = ATT_BLOCK
    n_chunks = tq // tk
    strict = lax.broadcasted_iota(jnp.int32, (tk, tk), 1) < lax.broadcasted_iota(jnp.int32, (tk, tk), 0)

    def run(units, base):
        zs = [None] * len(units)
        mids = [None] * len(units)

        def scores(i):
            h, r, kbs = units[i]
            zs[i] = _group_scores(qm_ref[h, r * tk : (r + 1) * tk, :], k_ref, base, kbs, False)

        def log_keep(i):
            ml = []
            for (_, diag), z in zip(units[i][2], zs[i]):
                sp = _softplus_sum(z)
                nk = jnp.where(strict, sp, 0.0) if diag else sp
                ml.append((z - sp, jnp.dot(nk.astype(BF16), tri_ref[...], preferred_element_type=F32),
                           jnp.sum(nk, axis=1, keepdims=True)))
            zs[i] = None
            mids[i] = ml

        def weigh(i):
            h, r, kbs = units[i]
            rows = slice(r * tk, (r + 1) * tk)
            c = c_ref[h, rows, :]
            ws = []
            for (_, diag), (lw, cs, rs) in zip(kbs, mids[i]):
                w = jnp.exp(lw - cs - c)
                ws.append((jnp.where(strict, w, 0.0) if diag else w).astype(BF16))
                c = c + rs
            mids[i] = None
            c_ref[h, rows, :] = c
            wcat = ws[0] if len(ws) == 1 else jnp.concatenate(ws[::-1], axis=1)
            acc_ref[h, rows, :] += jnp.dot(wcat, _key_rows(v_ref, base, kbs[-1][0], len(kbs)), preferred_element_type=F32)

        _skewed((scores, log_keep, weigh), len(units), SB_SKEW)

    def q_block(qi, carry):
        q0 = pl.multiple_of(qi * tq, tq)
        lm0 = _masked_q(q_ref, qm_ref, q0, tq, HEAD_DIM ** -0.5)
        c_ref[...] = jnp.zeros_like(c_ref)
        acc_ref[...] = jnp.zeros_like(acc_ref)
        run(_band_units(n_chunks, SB_KEY_GROUP), qi * n_chunks)

        def k_group(j, carry):
            run(_full_units(n_chunks, SB_KEY_GROUP), qi * n_chunks - 1 - j * SB_KEY_GROUP)
            return carry

        lax.fori_loop(0, qi * (n_chunks // SB_KEY_GROUP), k_group, 0)
        o_ref[pl.ds(q0, tq), :] = jnp.where(lm0, acc_ref[0], acc_ref[1]).astype(o_ref.dtype)
        return carry

    lax.fori_loop(0, q_ref.shape[0] // tq, q_block, 0)


def _tri():
    return jnp.asarray(np.tril(np.ones((ATT_BLOCK, ATT_BLOCK), np.float32), -1), BF16)


def _head_pair_specs(slen, n_pairs):
    return [pl.BlockSpec((None, slen, LANES), lambda b, p, o=o: (b, 0, o * n_pairs + p)) for o in range(3)]


def _sb_attention(qkv):
    bsz, slen, _ = qkv.shape
    n_pairs = ATT_WIDTH // LANES
    tq = min(ATT_Q_ROWS, slen)
    assert slen % tq == 0 and (slen == tq or (tq // ATT_BLOCK) % SB_KEY_GROUP == 0)
    return pl.pallas_call(
        functools.partial(_sb_kernel, tq),
        out_shape=jax.ShapeDtypeStruct((bsz, slen, ATT_WIDTH), BF16),
        grid=(bsz, n_pairs),
        in_specs=_head_pair_specs(slen, n_pairs) + [pl.BlockSpec((ATT_BLOCK, ATT_BLOCK), lambda b, p: (0, 0))],
        out_specs=pl.BlockSpec((None, slen, LANES), lambda b, p: (b, 0, p)),
        scratch_shapes=[pltpu.VMEM((2, tq, LANES), BF16), pltpu.VMEM((2, tq, LANES), F32), pltpu.VMEM((2, tq, LANES), F32)],
        compiler_params=_params(2),
        name="sb_attn",
    )(qkv, qkv, qkv, _tri())


def _fox_kernel(tq, q_ref, k_ref, v_ref, col_ref, row_ref, o_ref, qm_ref, ct_ref, m_ref, acc_ref):
    tk = ATT_BLOCK
    n_chunks = tq // tk
    pair = pl.program_id(1)
    causal = lax.broadcasted_iota(jnp.int32, (tk, tk), 1) <= lax.broadcasted_iota(jnp.int32, (tk, tk), 0)

    def run(units, base):
        zs = [None] * len(units)
        vms = {}

        def scores(i):
            h, r, kbs = units[i]
            zs[i] = _group_scores(qm_ref[h, r * tk : (r + 1) * tk, :], k_ref, base, kbs, True)

        def update(i):
            h, r, kbs = units[i]
            rows = slice(r * tk, (r + 1) * tk)
            ss = []
            for (off, diag), z in zip(kbs, zs[i]):
                s = z - row_ref[2 * pair + h, base + off]
                ss.append(jnp.where(causal, s, -jnp.inf) if diag else s)
            zs[i] = None
            top = functools.reduce(jnp.maximum, ss)
            ct = ct_ref[h, rows, :]
            m = m_ref[h, rows, :]
            m_new = jnp.maximum(m, jnp.max(top, axis=1, keepdims=True) + ct)
            shift = ct - m_new
            ps = [jnp.exp2(s + shift).astype(BF16) for s in ss]
            pcat = ps[0] if len(ps) == 1 else jnp.concatenate(ps[::-1], axis=1)
            key = (h, kbs[-1][0], len(kbs))
            if key not in vms:
                v = _key_rows(v_ref, base, kbs[-1][0], len(kbs))
                lane = lax.broadcasted_iota(jnp.int32, v.shape, 1)
                vms[key] = jnp.where(lane < HEAD_DIM if h == 0 else lane >= HEAD_DIM, v, jnp.ones_like(v))
            acc_ref[h, rows, :] = jnp.exp2(m - m_new) * acc_ref[h, rows, :] + jnp.dot(
                pcat, vms[key], preferred_element_type=F32)
            m_ref[h, rows, :] = m_new

        _skewed((scores, update), len(units), FOX_SKEW)

    def q_block(qi, carry):
        q0 = pl.multiple_of(qi * tq, tq)
        lm0 = _masked_q(q_ref, qm_ref, q0, tq, LOG2E * HEAD_DIM ** -0.5)
        col = col_ref[pl.ds(q0, tq), :]
        lane = lax.broadcasted_iota(jnp.int32, (tq, LANES), 1)
        for h in range(2):
            ct = jnp.sum(jnp.where(lane == 2 * pair + h, col, 0.0), axis=1, keepdims=True)
            ct_ref[h] = jnp.broadcast_to(ct, (tq, LANES))
        m_ref[...] = jnp.full(m_ref.shape, -jnp.inf, F32)
        acc_ref[...] = jnp.zeros_like(acc_ref)
        run(_band_units(n_chunks, FOX_KEY_GROUP), qi * n_chunks)

        def k_group(j, carry):
            run(_full_units(n_chunks, FOX_KEY_GROUP), qi * n_chunks - 1 - j * FOX_KEY_GROUP)
            return carry

        lax.fori_loop(0, qi * (n_chunks // FOX_KEY_GROUP), k_group, 0)
        acca = acc_ref[0]
        accb = acc_ref[1]
        out = jnp.where(lm0, acca / pltpu.roll(acca, HEAD_DIM, 1), accb / pltpu.roll(accb, HEAD_DIM, 1))
        o_ref[pl.ds(q0, tq), :] = out.astype(o_ref.dtype)
        return carry

    lax.fori_loop(0, q_ref.shape[0] // tq, q_block, 0)


def _fox_attention(qkv, cum_col, cum_row):
    bsz, slen, _ = qkv.shape
    n_pairs = ATT_WIDTH // LANES
    tq = min(ATT_Q_ROWS, slen)
    assert slen % tq == 0 and (slen == tq or (tq // ATT_BLOCK) % FOX_KEY_GROUP == 0)
    state = pltpu.VMEM((2, tq, LANES), F32)
    return pl.pallas_call(
        functools.partial(_fox_kernel, tq),
        out_shape=jax.ShapeDtypeStruct((bsz, slen, ATT_WIDTH), BF16),
        grid=(bsz, n_pairs),
        in_specs=_head_pair_specs(slen, n_pairs) + [
            pl.BlockSpec((None, slen, LANES), lambda b, p: (b, 0, 0)),
            pl.BlockSpec((None,) + cum_row.shape[1:], lambda b, p: (b, 0, 0, 0, 0)),
        ],
        out_specs=pl.BlockSpec((None, slen, LANES), lambda b, p: (b, 0, p)),
        scratch_shapes=[pltpu.VMEM((2, tq, LANES), BF16), state, state, state],
        compiler_params=_params(2),
        name="fox_attn",
    )(qkv, qkv, qkv, cum_col, cum_row)


def _mix_kernel(x_ref, ya_ref, yb_ref, yc_ref, mg_ref, mb_ref, mod_ref, wrg_ref, wsb_ref, wfox_ref, wo_ref, o_ref):
    d = x_ref.shape[1]
    mixed = None
    for n, (y_ref, w_ref) in enumerate(((ya_ref, wrg_ref), (yb_ref, wsb_ref), (yc_ref, wfox_ref))):
        cs = slice(n * d, (n + 1) * d)
        g = jax.nn.sigmoid(mg_ref[:, cs].astype(F32) + mb_ref[:, cs])
        t = g * jnp.dot(y_ref[...], w_ref[...], preferred_element_type=F32)
        mixed = t if mixed is None else mixed + t
    y = jnp.dot(mixed.astype(BF16), wo_ref[...], preferred_element_type=F32)
    o_ref[...] = x_ref[...] + (1.0 + mod_ref[5:6, :]) * y


def _mix(x, ya, yb, yc, mg, merge_b, mod, w_rg, w_sb, w_fox, w_o, tm):
    bsz, slen, d = x.shape
    row = lambda n: pl.BlockSpec((None, tm, n), lambda b, i: (b, i, 0))
    return pl.pallas_call(
        _mix_kernel,
        out_shape=jax.ShapeDtypeStruct(x.shape, F32),
        grid=(bsz, slen // tm),
        in_specs=[row(d), row(ya.shape[2]), row(yb.shape[2]), row(yc.shape[2]), row(mg.shape[2]),
                  pl.BlockSpec((1, mg.shape[2]), lambda b, i: (0, 0)), pl.BlockSpec((None, 9, d), lambda b, i: (b, 0, 0)),
                  _resident(w_rg.shape), _resident(w_sb.shape), _resident(w_fox.shape), _resident(w_o.shape)],
        out_specs=row(d),
        compiler_params=_params(2),
        name="mix",
    )(x, ya, yb, yc, mg, merge_b, mod, w_rg, w_sb, w_fox, w_o)


def _block_diag(w):
    n, bd, _ = w.shape
    per = RG_GROUP // bd
    eye = jnp.eye(per, dtype=w.dtype)
    wg = w.reshape(n // per, per, bd, bd)
    return jnp.einsum("gpde,pq->gpdqe", wg, eye).reshape(n // per, per * bd, per * bd).astype(BF16)


def kernel(x, c, ffn1_norm, ffn1_w1, ffn1_w3, ffn1_w2, mix_norm, w_in, conv_w, conv_b, rg_wa, rg_ba, rg_wx, rg_bx, rg_lam, fox_bf, merge_b, w_rg, w_sb, w_fox, w_o, ffn2_norm, ffn2_w1, ffn2_w3, ffn2_w2, ada_w, ada_b, final_norm, final_ada_w, final_ada_b):
    bsz, slen, d = x.shape
    depth = ada_w.shape[0]
    tm = min(512, slen)
    tc = 512
    mods = _ada(c, ada_w, ada_b).reshape(depth, bsz, 9, d)
    fm = _ada(c, final_ada_w[None], final_ada_b[None]).reshape(bsz, 2, d)
    cuts = np.cumsum((RG_WIDTH, RG_WIDTH, 3 * ATT_WIDTH, 3 * ATT_WIDTH, N_HEADS)).tolist()
    row = lambda v: v.reshape(1, -1)
    for l in range(depth):
        mod = mods[l]
        x = _ffn(x, row(ffn1_norm[l]), mod, ffn1_w1[l].astype(BF16), ffn1_w3[l].astype(BF16), ffn1_w2[l].astype(BF16),
                 row(final_norm), fm, 0, False, tm)
        wl = w_in[l].astype(BF16)
        w_f = jnp.pad(wl[:, cuts[3] : cuts[4]], ((0, 0), (0, LANES - N_HEADS)))
        ws = (wl[:, : cuts[0]], wl[:, cuts[0] : cuts[1]], wl[:, cuts[1] : cuts[2]], wl[:, cuts[2] : cuts[3]], w_f, wl[:, cuts[4] :])
        rgx, rgg, sb_qkv, fox_qkv, f, mg = _inproj(x, row(mix_norm[l]), mod, ws, tm)
        cum_col, cum_row = _fcum(f, jnp.pad(row(fox_bf[l]), ((0, 0), (0, LANES - N_HEADS))))
        cum_row = cum_row[:, :N_HEADS].reshape(bsz, N_HEADS, slen // ATT_BLOCK, 1, ATT_BLOCK)
        ya = _rglru(rgx, rgg, conv_w[l], row(conv_b[l]), _block_diag(rg_wa[l]), row(rg_ba[l]), _block_diag(rg_wx[l]),
                    row(rg_bx[l]), row(rg_lam[l]), tc)
        yb = _sb_attention(sb_qkv)
        yc = _fox_attention(fox_qkv, cum_col, cum_row)
        x = _mix(x, ya, yb, yc, mg, row(merge_b[l]), mod, w_rg[l].astype(BF16), w_sb[l].astype(BF16),
                 w_fox[l].astype(BF16), w_o[l].astype(BF16), tm)
        x = _ffn(x, row(ffn2_norm[l]), mod, ffn2_w1[l].astype(BF16), ffn2_w3[l].astype(BF16), ffn2_w2[l].astype(BF16),
                 row(final_norm), fm, 2, l == depth - 1, tm)
    return x
```

```python
import functools

import jax
import jax.numpy as jnp
import numpy as np
from jax import lax
from jax.experimental import pallas as pl
from jax.experimental.pallas import tpu as pltpu

F32 = jnp.float32
BF16 = jnp.bfloat16

D_MODEL = 1024
RG_WIDTH = 1024
RG_BLOCK_DIM = 64
CONV_WIDTH = 4
RG_C = 8.0
HEAD_DIM = 64
ATT_WIDTH = 512
N_HEADS = 8
D_FF = 2816
EPS = 1e-6

LANES = 128
SUBLANES = 8
MXU_DIM = 256
FF_CHUNK = MXU_DIM
ATT_BLOCK = 128
ATT_Q_ROWS = 1024
SB_KEY_GROUP = 2
FOX_KEY_GROUP = 4
SB_SKEW = 3
FOX_SKEW = 4
LOG2E = 1.4426950408889634
RG_GROUP = MXU_DIM
VMEM_LIMIT = 56 * 1024 * 1024


def _params(n_grid):
    return pltpu.CompilerParams(dimension_semantics=("parallel",) * n_grid, vmem_limit_bytes=VMEM_LIMIT)


def _resident(shape):
    nd = len(shape)
    return pl.BlockSpec(shape, lambda *_: (0,) * nd, pipeline_mode=pl.Buffered(1))


def _softplus(y):
    return jnp.maximum(y, 0.0) + jnp.log1p(jnp.exp(-jnp.abs(y)))


def _softplus_sum(y):
    return jnp.maximum(y, 0.0) + jnp.log(1.0 + jnp.exp2(jnp.abs(y) * -LOG2E))


def _norm_mod(x, gain, shift, scale):
    y = x * lax.rsqrt(jnp.mean(x * x, axis=-1, keepdims=True) + EPS)
    return (y * gain) * (1.0 + scale) + shift


def _ada_kernel(c_ref, w_ref, b_ref, o_ref):
    c = c_ref[...]
    o_ref[...] = jnp.dot(c * jax.nn.sigmoid(c), w_ref[...], preferred_element_type=F32) + b_ref[...]


def _ada(c, w, b):
    n_l, d, n = w.shape
    bsz = c.shape[0]
    tn = 1024
    return pl.pallas_call(
        _ada_kernel,
        out_shape=jax.ShapeDtypeStruct((n_l, bsz, n), F32),
        grid=(n_l, n // tn),
        in_specs=[
            pl.BlockSpec((bsz, d), lambda l, j: (0, 0)),
            pl.BlockSpec((None, d, tn), lambda l, j: (l, 0, j)),
            pl.BlockSpec((None, 1, tn), lambda l, j: (l, 0, j)),
        ],
        out_specs=pl.BlockSpec((None, bsz, tn), lambda l, j: (l, 0, j)),
        compiler_params=_params(2),
        name="ada",
    )(c, w, b.reshape(n_l, 1, n))


def _ffn_kernel(sub, last, x_ref, g_ref, mod_ref, w1_ref, w3_ref, w2_ref, fg_ref, fm_ref, o_ref, h_ref, act_ref):
    x = x_ref[...]
    h = _norm_mod(x, g_ref[...], mod_ref[3 * sub : 3 * sub + 1, :], mod_ref[3 * sub + 1 : 3 * sub + 2, :])
    h_ref[...] = h.astype(BF16)
    for c0 in range(0, w1_ref.shape[1], FF_CHUNK):
        cols = slice(c0, c0 + FF_CHUNK)
        a = jnp.dot(h_ref[...], w1_ref[:, cols], preferred_element_type=F32)
        b = jnp.dot(h_ref[...], w3_ref[:, cols], preferred_element_type=F32)
        act_ref[:, cols] = ((a * jax.nn.sigmoid(a)) * b).astype(BF16)
    y = jnp.dot(act_ref[...], w2_ref[...], preferred_element_type=F32)
    gate = mod_ref[3 * sub + 2 : 3 * sub + 3, :]
    out = x + (0.5 * (1.0 + gate)) * y
    if last:
        out = _norm_mod(out, fg_ref[...], fm_ref[0:1, :], fm_ref[1:2, :])
    o_ref[...] = out


def _ffn(x, gain, mod, w1, w3, w2, final_gain, fm, sub, last, tm):
    bsz, slen, d = x.shape
    tile = pl.BlockSpec((None, tm, d), lambda b, i: (b, i, 0))
    return pl.pallas_call(
        functools.partial(_ffn_kernel, sub, last),
        out_shape=jax.ShapeDtypeStruct(x.shape, F32),
        grid=(bsz, slen // tm),
        in_specs=[
            tile,
            pl.BlockSpec((1, d), lambda b, i: (0, 0)),
            pl.BlockSpec((None, 9, d), lambda b, i: (b, 0, 0)),
            _resident(w1.shape),
            _resident(w3.shape),
            _resident(w2.shape),
            pl.BlockSpec((1, d), lambda b, i: (0, 0)),
            pl.BlockSpec((None, 2, d), lambda b, i: (b, 0, 0)),
        ],
        out_specs=tile,
        scratch_shapes=[pltpu.VMEM((tm, d), BF16), pltpu.VMEM((tm, w2.shape[0]), BF16)],
        compiler_params=_params(2),
        name="ffn",
    )(x, gain, mod, w1, w3, w2, final_gain, fm)


def _inproj_kernel(x_ref, g_ref, mod_ref, wx_ref, wg_ref, wsb_ref, wfox_ref, wf_ref, wmg_ref,
                   rgx_ref, rgg_ref, sb_ref, fox_ref, f_ref, mg_ref, h_ref):
    h = _norm_mod(x_ref[...], g_ref[...], mod_ref[3:4, :], mod_ref[4:5, :])
    h_ref[...] = h.astype(BF16)
    for w_ref, o_ref in ((wx_ref, rgx_ref), (wg_ref, rgg_ref), (wsb_ref, sb_ref), (wfox_ref, fox_ref),
                         (wf_ref, f_ref), (wmg_ref, mg_ref)):
        n = w_ref.shape[1]
        step = min(n, 2 * MXU_DIM)
        for c0 in range(0, n, step):
            o_ref[:, c0 : c0 + step] = jnp.dot(
                h_ref[...], w_ref[:, c0 : c0 + step], preferred_element_type=F32).astype(o_ref.dtype)


def _inproj(x, gain, mod, ws, tm):
    bsz, slen, d = x.shape
    out_dtypes = (F32, F32, BF16, BF16, F32, BF16)
    row = lambda n: pl.BlockSpec((None, tm, n), lambda b, i: (b, i, 0))
    return pl.pallas_call(
        _inproj_kernel,
        out_shape=[jax.ShapeDtypeStruct((bsz, slen, w.shape[1]), dt) for w, dt in zip(ws, out_dtypes)],
        grid=(bsz, slen // tm),
        in_specs=[row(d), pl.BlockSpec((1, d), lambda b, i: (0, 0)), pl.BlockSpec((None, 9, d), lambda b, i: (b, 0, 0))]
        + [_resident(w.shape) for w in ws],
        out_specs=[row(w.shape[1]) for w in ws],
        scratch_shapes=[pltpu.VMEM((tm, d), BF16)],
        compiler_params=_params(2),
        name="inproj",
    )(x, gain, mod, *ws)


def _fcum_kernel(f_ref, bf_ref, col_ref, row_ref):
    slen = f_ref.shape[0]
    c = -_softplus(-(f_ref[...] + bf_ref[...]))
    t = lax.broadcasted_iota(jnp.int32, c.shape, 0)
    d = 1
    while d < slen:
        c = c + jnp.where(t >= d, pltpu.roll(c, d, 0), 0.0)
        d *= 2
    c = c * LOG2E
    col_ref[...] = c
    row_ref[...] = c.T


def _fcum(f, bf):
    bsz, slen, n = f.shape
    return pl.pallas_call(
        _fcum_kernel,
        out_shape=[jax.ShapeDtypeStruct((bsz, slen, n), F32), jax.ShapeDtypeStruct((bsz, n, slen), F32)],
        grid=(bsz,),
        in_specs=[pl.BlockSpec((None, slen, n), lambda b: (b, 0, 0)), pl.BlockSpec((1, n), lambda b: (0, 0))],
        out_specs=[pl.BlockSpec((None, slen, n), lambda b: (b, 0, 0)), pl.BlockSpec((None, n, slen), lambda b: (b, 0, 0))],
        compiler_params=_params(1),
        name="fcum",
    )(f, bf)


def _rglru_kernel(x_ref, gt_ref, cw_ref, cb_ref, wa_ref, ba_ref, wx_ref, bx_ref, lam_ref, o_ref,
                  xpad_ref, a_ref, u_ref):
    slen, tc = x_ref.shape
    pad = SUBLANES
    xpad_ref[0:pad, :] = jnp.zeros((pad, tc), F32)
    xpad_ref[pad : pad + slen, :] = x_ref[...]
    sp_lam = _softplus(-lam_ref[...])
    rc = slen // SUBLANES
    pitch = rc + SUBLANES
    for seg, r0 in enumerate(range(0, slen, rc)):
        ext = xpad_ref[r0 : r0 + pad + rc, :]
        xa = cb_ref[...] + cw_ref[CONV_WIDTH - 1 :, :] * ext[pad:, :]
        for back in range(1, CONV_WIDTH):
            xa = xa + cw_ref[CONV_WIDTH - 1 - back : CONV_WIDTH - back, :] * pltpu.roll(ext, back, 0)[pad:, :]
        xab = xa.astype(BF16)
        for g in range(tc // RG_GROUP):
            cs = slice(g * RG_GROUP, (g + 1) * RG_GROUP)
            r = jax.nn.sigmoid(jnp.dot(xab[:, cs], wa_ref[g], preferred_element_type=F32) + ba_ref[:, cs])
            i = jax.nn.sigmoid(jnp.dot(xab[:, cs], wx_ref[g], preferred_element_type=F32) + bx_ref[:, cs])
            log_a = (-RG_C * r) * sp_lam[:, cs]
            a = jnp.exp(log_a)
            y = -jnp.tanh(log_a) * (a * a + 1.0)
            root = jnp.where(y > 0.0, y * lax.rsqrt(y), 0.0)
            u = root * (i * xa[:, cs])
            for j in range(RG_GROUP // LANES):
                lanes = slice(j * LANES, (j + 1) * LANES)
                a_ref[g * (RG_GROUP // LANES) + j, seg * pitch : seg * pitch + rc, :] = a[:, lanes]
                u_ref[g * (RG_GROUP // LANES) + j, seg * pitch : seg * pitch + rc, :] = u[:, lanes]

    n_tiles = tc // LANES
    step = lambda t: pl.ds(t, SUBLANES, stride=pitch)
    zeros = (jnp.zeros((SUBLANES, LANES), F32),) * n_tiles
    ones = (jnp.ones((SUBLANES, LANES), F32),) * n_tiles

    def local_scan(t, st):
        hs, ps = [], []
        for j, (h, p) in enumerate(zip(*st)):
            a = a_ref[j, step(t), :]
            h = a * h + u_ref[j, step(t), :]
            p = a * p
            u_ref[j, step(t), :] = h
            a_ref[j, step(t), :] = p
            hs.append(h)
            ps.append(p)
        return tuple(hs), tuple(ps)

    h_end, a_all = lax.fori_loop(0, rc, local_scan, (zeros, ones), unroll=8)
    row = lax.broadcasted_iota(jnp.int32, (SUBLANES, LANES), 0)
    h_in = []
    for h, a in zip(h_end, a_all):
        for d in (1, 2, 4):
            a_s = jnp.where(row >= d, pltpu.roll(a, d, 0), 1.0)
            h_s = jnp.where(row >= d, pltpu.roll(h, d, 0), 0.0)
            h = h + a * h_s
            a = a * a_s
        h_in.append(jnp.where(row >= 1, pltpu.roll(h, 1, 0), 0.0))

    for seg, r0 in enumerate(range(0, slen, rc)):
        gate = jax.nn.gelu(gt_ref[r0 : r0 + rc, :], approximate=True)
        for j in range(n_tiles):
            lanes = slice(j * LANES, (j + 1) * LANES)
            rows = slice(seg * pitch, seg * pitch + rc)
            h = u_ref[j, rows, :] + a_ref[j, rows, :] * h_in[j][seg : seg + 1, :]
            o_ref[r0 : r0 + rc, lanes] = (gate[:, lanes] * h).astype(o_ref.dtype)


def _rglru(rgx, rgg, conv_w, conv_b, wa_bd, ba, wx_bd, bx, lam, tc):
    bsz, slen, width = rgx.shape
    tile = pl.BlockSpec((None, slen, tc), lambda b, j: (b, 0, j))
    vec = pl.BlockSpec((1, tc), lambda b, j: (0, j))
    wblk = pl.BlockSpec((tc // RG_GROUP, RG_GROUP, RG_GROUP), lambda b, j: (j, 0, 0))
    return pl.pallas_call(
        _rglru_kernel,
        out_shape=jax.ShapeDtypeStruct(rgx.shape, BF16),
        grid=(bsz, width // tc),
        in_specs=[tile, tile, pl.BlockSpec((CONV_WIDTH, tc), lambda b, j: (0, j)), vec, wblk, vec, wblk, vec, vec],
        out_specs=tile,
        scratch_shapes=[pltpu.VMEM((slen + SUBLANES, tc), F32)]
        + [pltpu.VMEM((tc // LANES, slen + SUBLANES * SUBLANES, LANES), F32)] * 2,
        compiler_params=_params(2),
        name="rglru",
    )(rgx, rgg, conv_w, conv_b, wa_bd, ba, wx_bd, bx, lam)


def _qk(qm, k):
    return lax.dot_general(qm, k, (((1,), (1,)), ((), ())), preferred_element_type=F32)


def _group_scores(qm, k_ref, base, kbs, fused):
    tk = ATT_BLOCK
    if not fused:
        return [_qk(qm, _key_rows(k_ref, base, off)) for off, _ in kbs]
    z = _qk(qm, _key_rows(k_ref, base, kbs[-1][0], len(kbs)))
    return [z[:, j * tk : (j + 1) * tk] for j in reversed(range(len(kbs)))]


def _masked_q(q_ref, qm_ref, q0, tq, scale):
    lm0 = lax.broadcasted_iota(jnp.int32, (tq, LANES), 1) < HEAD_DIM
    q = (q_ref[pl.ds(q0, tq), :].astype(F32) * scale).astype(BF16)
    qm_ref[0] = jnp.where(lm0, q, jnp.zeros_like(q))
    qm_ref[1] = jnp.where(lm0, jnp.zeros_like(q), q)
    return lm0


def _band_units(n_chunks, group):
    units = []
    for h in range(2):
        for r in range(n_chunks):
            kbs = [(d, d == r) for d in range(r, -1, -1)]
            units += [(h, r, kbs[i : i + group]) for i in range(0, len(kbs), group)]
    return units


def _full_units(n_chunks, group):
    kbs = [(-i, False) for i in range(group)]
    return [(h, r, kbs) for h in range(2) for r in range(n_chunks)]


def _skewed(stages, n, skew):
    for slot in range(n + (len(stages) - 1) * skew):
        for s, stage in enumerate(stages):
            if 0 <= slot - s * skew < n:
                stage(slot - s * skew)


def _key_rows(ref, base, off, n=1):
    tk = ATT_BLOCK
    return ref[pl.ds(pl.multiple_of((base + off) * tk, tk), n * tk), :]


def _sb_kernel(tq, q_ref, k_ref, v_ref, tri_ref, o_ref, qm_ref, c_ref, acc_ref):
    tk = ATT_BLOCK
    n_chunks = tq // tk
    strict = lax.broadcasted_iota(jnp.int32, (tk, tk), 1) < lax.broadcasted_iota(jnp.int32, (tk, tk), 0)

    def run(units, base):
        zs = [None] * len(units)
        mids = [None] * len(units)

        def scores(i):
            h, r, kbs = units[i]
            zs[i] = _group_scores(qm_ref[h, r * tk : (r + 1) * tk, :], k_ref, base, kbs, False)

        def log_keep(i):
            ml = []
            for (_, diag), z in zip(units[i][2], zs[i]):
                sp = _softplus_sum(z)
                nk = jnp.where(strict, sp, 0.0) if diag else sp
                ml.append((z - sp, jnp.dot(nk.astype(BF16), tri_ref[...], preferred_element_type=F32),
                           jnp.sum(nk, axis=1, keepdims=True)))
            zs[i] = None
            mids[i] = ml

        def weigh(i):
            h, r, kbs = units[i]
            rows = slice(r * tk, (r + 1) * tk)
            c = c_ref[h, rows, :]
            ws = []
            for (_, diag), (lw, cs, rs) in zip(kbs, mids[i]):
                w = jnp.exp(lw - cs - c)
                ws.append((jnp.where(strict, w, 0.0) if diag else w).astype(BF16))
                c = c + rs
            mids[i] = None
            c_ref[h, rows, :] = c
            wcat = ws[0] if len(ws) == 1 else jnp.concatenate(ws[::-1], axis=1)
            acc_ref[h, rows, :] += jnp.dot(wcat, _key_rows(v_ref, base, kbs[-1][0], len(kbs)), preferred_element_type=F32)

        _skewed((scores, log_keep, weigh), len(units), SB_SKEW)

    def q_block(qi, carry):
        q0 = pl.multiple_of(qi * tq, tq)
        lm0 = _masked_q(q_ref, qm_ref, q0, tq, HEAD_DIM ** -0.5)
        c_ref[...] = jnp.zeros_like(c_ref)
        acc_ref[...] = jnp.zeros_like(acc_ref)
        run(_band_units(n_chunks, SB_KEY_GROUP), qi * n_chunks)

        def k_group(j, carry):
            run(_full_units(n_chunks, SB_KEY_GROUP), qi * n_chunks - 1 - j * SB_KEY_GROUP)
            return carry

        lax.fori_loop(0, qi * (n_chunks // SB_KEY_GROUP), k_group, 0)
        o_ref[pl.ds(q0, tq), :] = jnp.where(lm0, acc_ref[0], acc_ref[1]).astype(o_ref.dtype)
        return carry

    lax.fori_loop(0, q_ref.shape[0] // tq, q_block, 0)


def _tri():
    return jnp.asarray(np.tril(np.ones((ATT_BLOCK, ATT_BLOCK), np.float32), -1), BF16)


def _head_pair_specs(slen, n_pairs):
    return [pl.BlockSpec((None, slen, LANES), lambda b, p, o=o: (b, 0, o * n_pairs + p)) for o in range(3)]


def _sb_attention(qkv):
    bsz, slen, _ = qkv.shape
    n_pairs = ATT_WIDTH // LANES
    tq = min(ATT_Q_ROWS, slen)
    assert slen % tq == 0 and (slen == tq or (tq // ATT_BLOCK) % SB_KEY_GROUP == 0)
    return pl.pallas_call(
        functools.partial(_sb_kernel, tq),
        out_shape=jax.ShapeDtypeStruct((bsz, slen, ATT_WIDTH), BF16),
        grid=(bsz, n_pairs),
        in_specs=_head_pair_specs(slen, n_pairs) + [pl.BlockSpec((ATT_BLOCK, ATT_BLOCK), lambda b, p: (0, 0))],
        out_specs=pl.BlockSpec((None, slen, LANES), lambda b, p: (b, 0, p)),
        scratch_shapes=[pltpu.VMEM((2, tq, LANES), BF16), pltpu.VMEM((2, tq, LANES), F32), pltpu.VMEM((2, tq, LANES), F32)],
        compiler_params=_params(2),
        name="sb_attn",
    )(qkv, qkv, qkv, _tri())


def _fox_kernel(tq, q_ref, k_ref, v_ref, col_ref, row_ref, o_ref, qm_ref, ct_ref, m_ref, acc_ref):
    tk = ATT_BLOCK
    n_chunks = tq // tk
    pair = pl.program_id(1)
    causal = lax.broadcasted_iota(jnp.int32, (tk, tk), 1) <= lax.broadcasted_iota(jnp.int32, (tk, tk), 0)

    def run(units, base):
        zs = [None] * len(units)
        vms = {}

        def scores(i):
            h, r, kbs = units[i]
            zs[i] = _group_scores(qm_ref[h, r * tk : (r + 1) * tk, :], k_ref, base, kbs, True)

        def update(i):
            h, r, kbs = units[i]
            rows = slice(r * tk, (r + 1) * tk)
            ss = []
            for (off, diag), z in zip(kbs, zs[i]):
                s = z - row_ref[2 * pair + h, base + off]
                ss.append(jnp.where(causal, s, -jnp.inf) if diag else s)
            zs[i] = None
            top = functools.reduce(jnp.maximum, ss)
            ct = ct_ref[h, rows, :]
            m = m_ref[h, rows, :]
            m_new = jnp.maximum(m, jnp.max(top, axis=1, keepdims=True) + ct)
            shift = ct - m_new
            ps = [jnp.exp2(s + shift).astype(BF16) for s in ss]
            pcat = ps[0] if len(ps) == 1 else jnp.concatenate(ps[::-1], axis=1)
            key = (h, kbs[-1][0], len(kbs))
            if key not in vms:
                v = _key_rows(v_ref, base, kbs[-1][0], len(kbs))
                lane = lax.broadcasted_iota(jnp.int32, v.shape, 1)
                vms[key] = jnp.where(lane < HEAD_DIM if h == 0 else lane >= HEAD_DIM, v, jnp.ones_like(v))
            acc_ref[h, rows, :] = jnp.exp2(m - m_new) * acc_ref[h, rows, :] + jnp.dot(
                pcat, vms[key], preferred_element_type=F32)
            m_ref[h, rows, :] = m_new

        _skewed((scores, update), len(units), FOX_SKEW)

    def q_block(qi, carry):
        q0 = pl.multiple_of(qi * tq, tq)
        lm0 = _masked_q(q_ref, qm_ref, q0, tq, LOG2E * HEAD_DIM ** -0.5)
        col = col_ref[pl.ds(q0, tq), :]
        lane = lax.broadcasted_iota(jnp.int32, (tq, LANES), 1)
        for h in range(2):
            ct = jnp.sum(jnp.where(lane == 2 * pair + h, col, 0.0), axis=1, keepdims=True)
            ct_ref[h] = jnp.broadcast_to(ct, (tq, LANES))
        m_ref[...] = jnp.full(m_ref.shape, -jnp.inf, F32)
        acc_ref[...] = jnp.zeros_like(acc_ref)
        run(_band_units(n_chunks, FOX_KEY_GROUP), qi * n_chunks)

        def k_group(j, carry):
            run(_full_units(n_chunks, FOX_KEY_GROUP), qi * n_chunks - 1 - j * FOX_KEY_GROUP)
            return carry

        lax.fori_loop(0, qi * (n_chunks // FOX_KEY_GROUP), k_group, 0)
        acca = acc_ref[0]
        accb = acc_ref[1]
        out = jnp.where(lm0, acca / pltpu.roll(acca, HEAD_DIM, 1), accb / pltpu.roll(accb, HEAD_DIM, 1))
        o_ref[pl.ds(q0, tq), :] = out.astype(o_ref.dtype)
        return carry

    lax.fori_loop(0, q_ref.shape[0] // tq, q_block, 0)


def _fox_attention(qkv, cum_col, cum_row):
    bsz, slen, _ = qkv.shape
    n_pairs = ATT_WIDTH // LANES
    tq = min(ATT_Q_ROWS, slen)
    assert slen % tq == 0 and (slen == tq or (tq // ATT_BLOCK) % FOX_KEY_GROUP == 0)
    state = pltpu.VMEM((2, tq, LANES), F32)
    return pl.pallas_call(
        functools.partial(_fox_kernel, tq),
        out_shape=jax.ShapeDtypeStruct((bsz, slen, ATT_WIDTH), BF16),
        grid=(bsz, n_pairs),
        in_specs=_head_pair_specs(slen, n_pairs) + [
            pl.BlockSpec((None, slen, LANES), lambda b, p: (b, 0, 0)),
            pl.BlockSpec((None,) + cum_row.shape[1:], lambda b, p: (b, 0, 0, 0, 0)),
        ],
        out_specs=pl.BlockSpec((None, slen, LANES), lambda b, p: (b, 0, p)),
        scratch_shapes=[pltpu.VMEM((2, tq, LANES), BF16), state, state, state],
        compiler_params=_params(2),
        name="fox_attn",
    )(qkv, qkv, qkv, cum_col, cum_row)


def _mix_kernel(x_ref, ya_ref, yb_ref, yc_ref, mg_ref, mb_ref, mod_ref, wrg_ref, wsb_ref, wfox_ref, wo_ref, o_ref):
    d = x_ref.shape[1]
    mixed = None
    for n, (y_ref, w_ref) in enumerate(((ya_ref, wrg_ref), (yb_ref, wsb_ref), (yc_ref, wfox_ref))):
        cs = slice(n * d, (n + 1) * d)
        g = jax.nn.sigmoid(mg_ref[:, cs].astype(F32) + mb_ref[:, cs])
        t = g * jnp.dot(y_ref[...], w_ref[...], preferred_element_type=F32)
        mixed = t if mixed is None else mixed + t
    y = jnp.dot(mixed.astype(BF16), wo_ref[...], preferred_element_type=F32)
    o_ref[...] = x_ref[...] + (1.0 + mod_ref[5:6, :]) * y


def _mix(x, ya, yb, yc, mg, merge_b, mod, w_rg, w_sb, w_fox, w_o, tm):
    bsz, slen, d = x.shape
    row = lambda n: pl.BlockSpec((None, tm, n), lambda b, i: (b, i, 0))
    return pl.pallas_call(
        _mix_kernel,
        out_shape=jax.ShapeDtypeStruct(x.shape, F32),
        grid=(bsz, slen // tm),
        in_specs=[row(d), row(ya.shape[2]), row(yb.shape[2]), row(yc.shape[2]), row(mg.shape[2]),
                  pl.BlockSpec((1, mg.shape[2]), lambda b, i: (0, 0)), pl.BlockSpec((None, 9, d), lambda b, i: (b, 0, 0)),
                  _resident(w_rg.shape), _resident(w_sb.shape), _resident(w_fox.shape), _resident(w_o.shape)],
        out_specs=row(d),
        compiler_params=_params(2),
        name="mix",
    )(x, ya, yb, yc, mg, merge_b, mod, w_rg, w_sb, w_fox, w_o)


def _block_diag(w):
    n, bd, _ = w.shape
    per = RG_GROUP // bd
    eye = jnp.eye(per, dtype=w.dtype)
    wg = w.reshape(n // per, per, bd, bd)
    return jnp.einsum("gpde,pq->gpdqe", wg, eye).reshape(n // per, per * bd, per * bd).astype(BF16)


def kernel(x, c, ffn1_norm, ffn1_w1, ffn1_w3, ffn1_w2, mix_norm, w_in, conv_w, conv_b, rg_wa, rg_ba, rg_wx, rg_bx, rg_lam, fox_bf, merge_b, w_rg, w_sb, w_fox, w_o, ffn2_norm, ffn2_w1, ffn2_w3, ffn2_w2, ada_w, ada_b, final_norm, final_ada_w, final_ada_b):
    bsz, slen, d = x.shape
    depth = ada_w.shape[0]
    tm = min(512, slen)
    tc = 512
    mods = _ada(c, ada_w, ada_b).reshape(depth, bsz, 9, d)
    fm = _ada(c, final_ada_w[None], final_ada_b[None]).reshape(bsz, 2, d)
    cuts = np.cumsum((RG_WIDTH, RG_WIDTH, 3 * ATT_WIDTH, 3 * ATT_WIDTH, N_HEADS)).tolist()
    row = lambda v: v.reshape(1, -1)
    for l in range(depth):
        mod = mods[l]
        x = _ffn(x, row(ffn1_norm[l]), mod, ffn1_w1[l].astype(BF16), ffn1_w3[l].astype(BF16), ffn1_w2[l].astype(BF16),
                 row(final_norm), fm, 0, False, tm)
        wl = w_in[l].astype(BF16)
        w_f = jnp.pad(wl[:, cuts[3] : cuts[4]], ((0, 0), (0, LANES - N_HEADS)))
        ws = (wl[:, : cuts[0]], wl[:, cuts[0] : cuts[1]], wl[:, cuts[1] : cuts[2]], wl[:, cuts[2] : cuts[3]], w_f, wl[:, cuts[4] :])
        rgx, rgg, sb_qkv, fox_qkv, f, mg = _inproj(x, row(mix_norm[l]), mod, ws, tm)
        cum_col, cum_row = _fcum(f, jnp.pad(row(fox_bf[l]), ((0, 0), (0, LANES - N_HEADS))))
        cum_row = cum_row[:, :N_HEADS].reshape(bsz, N_HEADS, slen // ATT_BLOCK, 1, ATT_BLOCK)
        ya = _rglru(rgx, rgg, conv_w[l], row(conv_b[l]), _block_diag(rg_wa[l]), row(rg_ba[l]), _block_diag(rg_wx[l]),
                    row(rg_bx[l]), row(rg_lam[l]), tc)
        yb = _sb_attention(sb_qkv)
        yc = _fox_attention(fox_qkv, cum_col, cum_row)
        x = _mix(x, ya, yb, yc, mg, row(merge_b[l]), mod, w_rg[l].astype(BF16), w_sb[l].astype(BF16),
                 w_fox[l].astype(BF16), w_o[l].astype(BF16), tm)
        x = _ffn(x, row(ffn2_norm[l]), mod, ffn2_w1[l].astype(BF16), ffn2_w3[l].astype(BF16), ffn2_w2[l].astype(BF16),
                 row(final_norm), fm, 2, l == depth - 1, tm)
    return x
```

```python
import functools

import jax
import jax.numpy as jnp
import numpy as np
from jax import lax
from jax.experimental import pallas as pl
from jax.experimental.pallas import tpu as pltpu

F32 = jnp.float32
BF16 = jnp.bfloat16

D_MODEL = 1024
RG_WIDTH = 1024
RG_BLOCK_DIM = 64
CONV_WIDTH = 4
RG_C = 8.0
HEAD_DIM = 64
ATT_WIDTH = 512
N_HEADS = 8
D_FF = 2816
EPS = 1e-6

LANES = 128
SUBLANES = 8
MXU_DIM = 256
FF_CHUNK = MXU_DIM
ATT_BLOCK = 128
SB_Q_ROWS = 2048
FOX_Q_ROWS = 1024
SB_KEY_GROUP = 2
FOX_KEY_GROUP = 4
SB_SKEW = 3
FOX_SKEW = 4
LOG2E = 1.4426950408889634
RG_GROUP = MXU_DIM
VMEM_LIMIT = 56 * 1024 * 1024


def _params(n_grid):
    return pltpu.CompilerParams(dimension_semantics=("parallel",) * n_grid, vmem_limit_bytes=VMEM_LIMIT)


def _resident(shape):
    nd = len(shape)
    return pl.BlockSpec(shape, lambda *_: (0,) * nd, pipeline_mode=pl.Buffered(1))


def _softplus(y):
    return jnp.maximum(y, 0.0) + jnp.log1p(jnp.exp(-jnp.abs(y)))


def _softplus_sum(y):
    return jnp.maximum(y, 0.0) + jnp.log(1.0 + jnp.exp2(jnp.abs(y) * -LOG2E))


def _norm_mod(x, gain, shift, scale):
    y = x * lax.rsqrt(jnp.mean(x * x, axis=-1, keepdims=True) + EPS)
    return (y * gain) * (1.0 + scale) + shift


def _ada_kernel(c_ref, w_ref, b_ref, o_ref):
    c = c_ref[...]
    o_ref[...] = jnp.dot(c * jax.nn.sigmoid(c), w_ref[...], preferred_element_type=F32) + b_ref[...]


def _ada(c, w, b):
    n_l, d, n = w.shape
    bsz = c.shape[0]
    tn = 1024
    return pl.pallas_call(
        _ada_kernel,
        out_shape=jax.ShapeDtypeStruct((n_l, bsz, n), F32),
        grid=(n_l, n // tn),
        in_specs=[
            pl.BlockSpec((bsz, d), lambda l, j: (0, 0)),
            pl.BlockSpec((None, d, tn), lambda l, j: (l, 0, j)),
            pl.BlockSpec((None, 1, tn), lambda l, j: (l, 0, j)),
        ],
        out_specs=pl.BlockSpec((None, bsz, tn), lambda l, j: (l, 0, j)),
        compiler_params=_params(2),
        name="ada",
    )(c, w, b.reshape(n_l, 1, n))


def _ffn_kernel(sub, last, x_ref, g_ref, mod_ref, w1_ref, w3_ref, w2_ref, fg_ref, fm_ref, o_ref, h_ref, act_ref):
    x = x_ref[...]
    h = _norm_mod(x, g_ref[...], mod_ref[3 * sub : 3 * sub + 1, :], mod_ref[3 * sub + 1 : 3 * sub + 2, :])
    h_ref[...] = h.astype(BF16)
    for c0 in range(0, w1_ref.shape[1], FF_CHUNK):
        cols = slice(c0, c0 + FF_CHUNK)
        a = jnp.dot(h_ref[...], w1_ref[:, cols], preferred_element_type=F32)
        b = jnp.dot(h_ref[...], w3_ref[:, cols], preferred_element_type=F32)
        act_ref[:, cols] = ((a * jax.nn.sigmoid(a)) * b).astype(BF16)
    y = jnp.dot(act_ref[...], w2_ref[...], preferred_element_type=F32)
    gate = mod_ref[3 * sub + 2 : 3 * sub + 3, :]
    out = x + (0.5 * (1.0 + gate)) * y
    if last:
        out = _norm_mod(out, fg_ref[...], fm_ref[0:1, :], fm_ref[1:2, :])
    o_ref[...] = out


def _ffn(x, gain, mod, w1, w3, w2, final_gain, fm, sub, last, tm):
    bsz, slen, d = x.shape
    tile = pl.BlockSpec((None, tm, d), lambda b, i: (b, i, 0))
    return pl.pallas_call(
        functools.partial(_ffn_kernel, sub, last),
        out_shape=jax.ShapeDtypeStruct(x.shape, F32),
        grid=(bsz, slen // tm),
        in_specs=[
            tile,
            pl.BlockSpec((1, d), lambda b, i: (0, 0)),
            pl.BlockSpec((None, 9, d), lambda b, i: (b, 0, 0)),
            _resident(w1.shape),
            _resident(w3.shape),
            _resident(w2.shape),
            pl.BlockSpec((1, d), lambda b, i: (0, 0)),
            pl.BlockSpec((None, 2, d), lambda b, i: (b, 0, 0)),
        ],
        out_specs=tile,
        scratch_shapes=[pltpu.VMEM((tm, d), BF16), pltpu.VMEM((tm, w2.shape[0]), BF16)],
        compiler_params=_params(2),
        name="ffn",
    )(x, gain, mod, w1, w3, w2, final_gain, fm)


def _inproj_kernel(x_ref, g_ref, mod_ref, wx_ref, wg_ref, wsb_ref, wfox_ref, wf_ref, wmg_ref,
                   rgx_ref, rgg_ref, sb_ref, fox_ref, f_ref, mg_ref, h_ref):
    h = _norm_mod(x_ref[...], g_ref[...], mod_ref[3:4, :], mod_ref[4:5, :])
    h_ref[...] = h.astype(BF16)
    for w_ref, o_ref in ((wx_ref, rgx_ref), (wg_ref, rgg_ref), (wsb_ref, sb_ref), (wfox_ref, fox_ref),
                         (wf_ref, f_ref), (wmg_ref, mg_ref)):
        n = w_ref.shape[1]
        step = min(n, 2 * MXU_DIM)
        for c0 in range(0, n, step):
            o_ref[:, c0 : c0 + step] = jnp.dot(
                h_ref[...], w_ref[:, c0 : c0 + step], preferred_element_type=F32).astype(o_ref.dtype)


def _inproj(x, gain, mod, ws, tm):
    bsz, slen, d = x.shape
    out_dtypes = (F32, F32, BF16, BF16, F32, BF16)
    row = lambda n: pl.BlockSpec((None, tm, n), lambda b, i: (b, i, 0))
    return pl.pallas_call(
        _inproj_kernel,
        out_shape=[jax.ShapeDtypeStruct((bsz, slen, w.shape[1]), dt) for w, dt in zip(ws, out_dtypes)],
        grid=(bsz, slen // tm),
        in_specs=[row(d), pl.BlockSpec((1, d), lambda b, i: (0, 0)), pl.BlockSpec((None, 9, d), lambda b, i: (b, 0, 0))]
        + [_resident(w.shape) for w in ws],
        out_specs=[row(w.shape[1]) for w in ws],
        scratch_shapes=[pltpu.VMEM((tm, d), BF16)],
        compiler_params=_params(2),
        name="inproj",
    )(x, gain, mod, *ws)


def _fcum_kernel(f_ref, bf_ref, col_ref, row_ref):
    slen = f_ref.shape[0]
    c = -_softplus(-(f_ref[...] + bf_ref[...]))
    t = lax.broadcasted_iota(jnp.int32, c.shape, 0)
    d = 1
    while d < slen:
        c = c + jnp.where(t >= d, pltpu.roll(c, d, 0), 0.0)
        d *= 2
    c = c * LOG2E
    col_ref[...] = c
    row_ref[...] = c.T


def _fcum(f, bf):
    bsz, slen, n = f.shape
    return pl.pallas_call(
        _fcum_kernel,
        out_shape=[jax.ShapeDtypeStruct((bsz, slen, n), F32), jax.ShapeDtypeStruct((bsz, n, slen), F32)],
        grid=(bsz,),
        in_specs=[pl.BlockSpec((None, slen, n), lambda b: (b, 0, 0)), pl.BlockSpec((1, n), lambda b: (0, 0))],
        out_specs=[pl.BlockSpec((None, slen, n), lambda b: (b, 0, 0)), pl.BlockSpec((None, n, slen), lambda b: (b, 0, 0))],
        compiler_params=_params(1),
        name="fcum",
    )(f, bf)


def _rglru_kernel(x_ref, gt_ref, cw_ref, cb_ref, wa_ref, ba_ref, wx_ref, bx_ref, lam_ref, o_ref,
                  xpad_ref, a_ref, u_ref):
    slen, tc = x_ref.shape
    pad = SUBLANES
    xpad_ref[0:pad, :] = jnp.zeros((pad, tc), F32)
    xpad_ref[pad : pad + slen, :] = x_ref[...]
    sp_lam = _softplus(-lam_ref[...])
    rc = slen // SUBLANES
    pitch = rc + SUBLANES
    for seg, r0 in enumerate(range(0, slen, rc)):
        ext = xpad_ref[r0 : r0 + pad + rc, :]
        xa = cb_ref[...] + cw_ref[CONV_WIDTH - 1 :, :] * ext[pad:, :]
        for back in range(1, CONV_WIDTH):
            xa = xa + cw_ref[CONV_WIDTH - 1 - back : CONV_WIDTH - back, :] * pltpu.roll(ext, back, 0)[pad:, :]
        xab = xa.astype(BF16)
        for g in range(tc // RG_GROUP):
            cs = slice(g * RG_GROUP, (g + 1) * RG_GROUP)
            r = jax.nn.sigmoid(jnp.dot(xab[:, cs], wa_ref[g], preferred_element_type=F32) + ba_ref[:, cs])
            i = jax.nn.sigmoid(jnp.dot(xab[:, cs], wx_ref[g], preferred_element_type=F32) + bx_ref[:, cs])
            log_a = (-RG_C * r) * sp_lam[:, cs]
            a = jnp.exp(log_a)
            y = -jnp.tanh(log_a) * (a * a + 1.0)
            root = jnp.where(y > 0.0, y * lax.rsqrt(y), 0.0)
            u = root * (i * xa[:, cs])
            for j in range(RG_GROUP // LANES):
                lanes = slice(j * LANES, (j + 1) * LANES)
                a_ref[g * (RG_GROUP // LANES) + j, seg * pitch : seg * pitch + rc, :] = a[:, lanes]
                u_ref[g * (RG_GROUP // LANES) + j, seg * pitch : seg * pitch + rc, :] = u[:, lanes]

    n_tiles = tc // LANES
    step = lambda t: pl.ds(t, SUBLANES, stride=pitch)
    zeros = (jnp.zeros((SUBLANES, LANES), F32),) * n_tiles
    ones = (jnp.ones((SUBLANES, LANES), F32),) * n_tiles

    def local_scan(t, st):
        hs, ps = [], []
        for j, (h, p) in enumerate(zip(*st)):
            a = a_ref[j, step(t), :]
            h = a * h + u_ref[j, step(t), :]
            p = a * p
            u_ref[j, step(t), :] = h
            a_ref[j, step(t), :] = p
            hs.append(h)
            ps.append(p)
        return tuple(hs), tuple(ps)

    h_end, a_all = lax.fori_loop(0, rc, local_scan, (zeros, ones), unroll=8)
    row = lax.broadcasted_iota(jnp.int32, (SUBLANES, LANES), 0)
    h_in = []
    for h, a in zip(h_end, a_all):
        for d in (1, 2, 4):
            a_s = jnp.where(row >= d, pltpu.roll(a, d, 0), 1.0)
            h_s = jnp.where(row >= d, pltpu.roll(h, d, 0), 0.0)
            h = h + a * h_s
            a = a * a_s
        h_in.append(jnp.where(row >= 1, pltpu.roll(h, 1, 0), 0.0))

    for seg, r0 in enumerate(range(0, slen, rc)):
        gate = jax.nn.gelu(gt_ref[r0 : r0 + rc, :], approximate=True)
        for j in range(n_tiles):
            lanes = slice(j * LANES, (j + 1) * LANES)
            rows = slice(seg * pitch, seg * pitch + rc)
            h = u_ref[j, rows, :] + a_ref[j, rows, :] * h_in[j][seg : seg + 1, :]
            o_ref[r0 : r0 + rc, lanes] = (gate[:, lanes] * h).astype(o_ref.dtype)


def _rglru(rgx, rgg, conv_w, conv_b, wa_bd, ba, wx_bd, bx, lam, tc):
    bsz, slen, width = rgx.shape
    tile = pl.BlockSpec((None, slen, tc), lambda b, j: (b, 0, j))
    vec = pl.BlockSpec((1, tc), lambda b, j: (0, j))
    wblk = pl.BlockSpec((tc // RG_GROUP, RG_GROUP, RG_GROUP), lambda b, j: (j, 0, 0))
    return pl.pallas_call(
        _rglru_kernel,
        out_shape=jax.ShapeDtypeStruct(rgx.shape, BF16),
        grid=(bsz, width // tc),
        in_specs=[tile, tile, pl.BlockSpec((CONV_WIDTH, tc), lambda b, j: (0, j)), vec, wblk, vec, wblk, vec, vec],
        out_specs=tile,
        scratch_shapes=[pltpu.VMEM((slen + SUBLANES, tc), F32)]
        + [pltpu.VMEM((tc // LANES, slen + SUBLANES * SUBLANES, LANES), F32)] * 2,
        compiler_params=_params(2),
        name="rglru",
    )(rgx, rgg, conv_w, conv_b, wa_bd, ba, wx_bd, bx, lam)


def _qk(qm, k):
    return lax.dot_general(qm, k, (((1,), (1,)), ((), ())), preferred_element_type=F32)


def _group_scores(qm, k_ref, base, kbs, fused):
    tk = ATT_BLOCK
    if not fused:
        return [_qk(qm, _key_rows(k_ref, base, off)) for off, _ in kbs]
    z = _qk(qm, _key_rows(k_ref, base, kbs[-1][0], len(kbs)))
    return [z[:, j * tk : (j + 1) * tk] for j in reversed(range(len(kbs)))]


def _masked_q(q_ref, qm_ref, q0, tq, scale):
    lm0 = lax.broadcasted_iota(jnp.int32, (tq, LANES), 1) < HEAD_DIM
    q = (q_ref[pl.ds(q0, tq), :].astype(F32) * scale).astype(BF16)
    qm_ref[0] = jnp.where(lm0, q, jnp.zeros_like(q))
    qm_ref[1] = jnp.where(lm0, jnp.zeros_like(q), q)
    return lm0


def _band_units(n_chunks, group):
    units = []
    for h in range(2):
        for r in range(n_chunks):
            kbs = [(d, d == r) for d in range(r, -1, -1)]
            units += [(h, r, kbs[i : i + group]) for i in range(0, len(kbs), group)]
    return units


def _full_units(n_chunks, group):
    kbs = [(-i, False) for i in range(group)]
    return [(h, r, kbs) for h in range(2) for r in range(n_chunks)]


def _skewed(stages, n, skew):
    for slot in range(n + (len(stages) - 1) * skew):
        for s, stage in enumerate(stages):
            if 0 <= slot - s * skew < n:
                stage(slot - s * skew)


def _key_rows(ref, base, off, n=1):
    tk = ATT_BLOCK
    return ref[pl.ds(pl.multiple_of((base + off) * tk, tk), n * tk), :]


def _sb_kernel(tq, q_ref, k_ref, v_ref, tri_ref, o_ref, qm_ref, c_ref, acc_ref):
    tk = ATT_BLOCK
    n_chunks = tq // tk
    strict = lax.broadcasted_iota(jnp.int32, (tk, tk), 1) < lax.broadcasted_iota(jnp.int32, (tk, tk), 0)

    def run(units, base):
        zs = [None] * len(units)
        mids = [None] * len(units)

        def scores(i):
            h, r, kbs = units[i]
            zs[i] = _group_scores(qm_ref[h, r * tk : (r + 1) * tk, :], k_ref, base, kbs, False)

        def log_keep(i):
            ml = []
            for (_, diag), z in zip(units[i][2], zs[i]):
                sp = _softplus_sum(z)
                nk = jnp.where(strict, sp, 0.0) if diag else sp
                ml.append((z - sp, jnp.dot(nk.astype(BF16), tri_ref[...], preferred_element_type=F32),
                           jnp.sum(nk, axis=1, keepdims=True)))
            zs[i] = None
            mids[i] = ml

        def weigh(i):
            h, r, kbs = units[i]
            rows = slice(r * tk, (r + 1) * tk)
            c = c_ref[h, rows, :]
            ws = []
            for (_, diag), (lw, cs, rs) in zip(kbs, mids[i]):
                w = jnp.exp(lw - cs - c)
                ws.append((jnp.where(strict, w, 0.0) if diag else w).astype(BF16))
                c = c + rs
            mids[i] = None
            c_ref[h, rows, :] = c
            wcat = ws[0] if len(ws) == 1 else jnp.concatenate(ws[::-1], axis=1)
            acc_ref[h, rows, :] += jnp.dot(wcat, _key_rows(v_ref, base, kbs[-1][0], len(kbs)), preferred_element_type=F32)

        _skewed((scores, log_keep, weigh), len(units), SB_SKEW)

    def q_block(qi, carry):
        q0 = pl.multiple_of(qi * tq, tq)
        lm0 = _masked_q(q_ref, qm_ref, q0, tq, HEAD_DIM ** -0.5)
        c_ref[...] = jnp.zeros_like(c_ref)
        acc_ref[...] = jnp.zeros_like(acc_ref)
        run(_band_units(n_chunks, SB_KEY_GROUP), qi * n_chunks)

        def k_group(j, carry):
            run(_full_units(n_chunks, SB_KEY_GROUP), qi * n_chunks - 1 - j * SB_KEY_GROUP)
            return carry

        lax.fori_loop(0, qi * (n_chunks // SB_KEY_GROUP), k_group, 0)
        o_ref[pl.ds(q0, tq), :] = jnp.where(lm0, acc_ref[0], acc_ref[1]).astype(o_ref.dtype)
        return carry

    lax.fori_loop(0, q_ref.shape[0] // tq, q_block, 0)


def _tri():
    return jnp.asarray(np.tril(np.ones((ATT_BLOCK, ATT_BLOCK), np.float32), -1), BF16)


def _head_pair_specs(slen, n_pairs):
    return [pl.BlockSpec((None, slen, LANES), lambda b, p, o=o: (b, 0, o * n_pairs + p)) for o in range(3)]


def _sb_attention(qkv):
    bsz, slen, _ = qkv.shape
    n_pairs = ATT_WIDTH // LANES
    tq = min(SB_Q_ROWS, slen)
    assert slen % tq == 0 and (slen == tq or (tq // ATT_BLOCK) % SB_KEY_GROUP == 0)
    return pl.pallas_call(
        functools.partial(_sb_kernel, tq),
        out_shape=jax.ShapeDtypeStruct((bsz, slen, ATT_WIDTH), BF16),
        grid=(bsz, n_pairs),
        in_specs=_head_pair_specs(slen, n_pairs) + [pl.BlockSpec((ATT_BLOCK, ATT_BLOCK), lambda b, p: (0, 0))],
        out_specs=pl.BlockSpec((None, slen, LANES), lambda b, p: (b, 0, p)),
        scratch_shapes=[pltpu.VMEM((2, tq, LANES), BF16), pltpu.VMEM((2, tq, LANES), F32), pltpu.VMEM((2, tq, LANES), F32)],
        compiler_params=_params(2),
        name="sb_attn",
    )(qkv, qkv, qkv, _tri())


def _fox_kernel(tq, q_ref, k_ref, v_ref, col_ref, row_ref, o_ref, qm_ref, ct_ref, m_ref, acc_ref):
    tk = ATT_BLOCK
    n_chunks = tq // tk
    pair = pl.program_id(1)
    causal = lax.broadcasted_iota(jnp.int32, (tk, tk), 1) <= lax.broadcasted_iota(jnp.int32, (tk, tk), 0)

    def run(units, base):
        zs = [None] * len(units)
        vms = {}

        def scores(i):
            h, r, kbs = units[i]
            zs[i] = _group_scores(qm_ref[h, r * tk : (r + 1) * tk, :], k_ref, base, kbs, True)

        def update(i):
            h, r, kbs = units[i]
            rows = slice(r * tk, (r + 1) * tk)
            ss = []
            for (off, diag), z in zip(kbs, zs[i]):
                s = z - row_ref[2 * pair + h, base + off]
                ss.append(jnp.where(causal, s, -jnp.inf) if diag else s)
            zs[i] = None
            top = functools.reduce(jnp.maximum, ss)
            ct = ct_ref[h, rows, :]
            m = m_ref[h, rows, :]
            m_new = jnp.maximum(m, jnp.max(top, axis=1, keepdims=True) + ct)
            shift = ct - m_new
            ps = [jnp.exp2(s + shift).astype(BF16) for s in ss]
            pcat = ps[0] if len(ps) == 1 else jnp.concatenate(ps[::-1], axis=1)
            key = (h, kbs[-1][0], len(kbs))
            if key not in vms:
                v = _key_rows(v_ref, base, kbs[-1][0], len(kbs))
                lane = lax.broadcasted_iota(jnp.int32, v.shape, 1)
                vms[key] = jnp.where(lane < HEAD_DIM if h == 0 else lane >= HEAD_DIM, v, jnp.ones_like(v))
            acc_ref[h, rows, :] = jnp.exp2(m - m_new) * acc_ref[h, rows, :] + jnp.dot(
                pcat, vms[key], preferred_element_type=F32)
            m_ref[h, rows, :] = m_new

        _skewed((scores, update), len(units), FOX_SKEW)

    def q_block(qi, carry):
        q0 = pl.multiple_of(qi * tq, tq)
        lm0 = _masked_q(q_ref, qm_ref, q0, tq, LOG2E * HEAD_DIM ** -0.5)
        col = col_ref[pl.ds(q0, tq), :]
        lane = lax.broadcasted_iota(jnp.int32, (tq, LANES), 1)
        for h in range(2):
            ct = jnp.sum(jnp.where(lane == 2 * pair + h, col, 0.0), axis=1, keepdims=True)
            ct_ref[h] = jnp.broadcast_to(ct, (tq, LANES))
        m_ref[...] = jnp.full(m_ref.shape, -jnp.inf, F32)
        acc_ref[...] = jnp.zeros_like(acc_ref)
        run(_band_units(n_chunks, FOX_KEY_GROUP), qi * n_chunks)

        def k_group(j, carry):
            run(_full_units(n_chunks, FOX_KEY_GROUP), qi * n_chunks - 1 - j * FOX_KEY_GROUP)
            return carry

        lax.fori_loop(0, qi * (n_chunks // FOX_KEY_GROUP), k_group, 0)
        acca = acc_ref[0]
        accb = acc_ref[1]
        out = jnp.where(lm0, acca / pltpu.roll(acca, HEAD_DIM, 1), accb / pltpu.roll(accb, HEAD_DIM, 1))
        o_ref[pl.ds(q0, tq), :] = out.astype(o_ref.dtype)
        return carry

    lax.fori_loop(0, q_ref.shape[0] // tq, q_block, 0)


def _fox_attention(qkv, cum_col, cum_row):
    bsz, slen, _ = qkv.shape
    n_pairs = ATT_WIDTH // LANES
    tq = min(FOX_Q_ROWS, slen)
    assert slen % tq == 0 and (slen == tq or (tq // ATT_BLOCK) % FOX_KEY_GROUP == 0)
    state = pltpu.VMEM((2, tq, LANES), F32)
    return pl.pallas_call(
        functools.partial(_fox_kernel, tq),
        out_shape=jax.ShapeDtypeStruct((bsz, slen, ATT_WIDTH), BF16),
        grid=(bsz, n_pairs),
        in_specs=_head_pair_specs(slen, n_pairs) + [
            pl.BlockSpec((None, slen, LANES), lambda b, p: (b, 0, 0)),
            pl.BlockSpec((None,) + cum_row.shape[1:], lambda b, p: (b, 0, 0, 0, 0)),
        ],
        out_specs=pl.BlockSpec((None, slen, LANES), lambda b, p: (b, 0, p)),
        scratch_shapes=[pltpu.VMEM((2, tq, LANES), BF16), state, state, state],
        compiler_params=_params(2),
        name="fox_attn",
    )(qkv, qkv, qkv, cum_col, cum_row)


def _mix_kernel(x_ref, ya_ref, yb_ref, yc_ref, mg_ref, mb_ref, mod_ref, wrg_ref, wsb_ref, wfox_ref, wo_ref, o_ref):
    d = x_ref.shape[1]
    mixed = None
    for n, (y_ref, w_ref) in enumerate(((ya_ref, wrg_ref), (yb_ref, wsb_ref), (yc_ref, wfox_ref))):
        cs = slice(n * d, (n + 1) * d)
        g = jax.nn.sigmoid(mg_ref[:, cs].astype(F32) + mb_ref[:, cs])
        t = g * jnp.dot(y_ref[...], w_ref[...], preferred_element_type=F32)
        mixed = t if mixed is None else mixed + t
    y = jnp.dot(mixed.astype(BF16), wo_ref[...], preferred_element_type=F32)
    o_ref[...] = x_ref[...] + (1.0 + mod_ref[5:6, :]) * y


def _mix(x, ya, yb, yc, mg, merge_b, mod, w_rg, w_sb, w_fox, w_o, tm):
    bsz, slen, d = x.shape
    row = lambda n: pl.BlockSpec((None, tm, n), lambda b, i: (b, i, 0))
    return pl.pallas_call(
        _mix_kernel,
        out_shape=jax.ShapeDtypeStruct(x.shape, F32),
        grid=(bsz, slen // tm),
        in_specs=[row(d), row(ya.shape[2]), row(yb.shape[2]), row(yc.shape[2]), row(mg.shape[2]),
                  pl.BlockSpec((1, mg.shape[2]), lambda b, i: (0, 0)), pl.BlockSpec((None, 9, d), lambda b, i: (b, 0, 0)),
                  _resident(w_rg.shape), _resident(w_sb.shape), _resident(w_fox.shape), _resident(w_o.shape)],
        out_specs=row(d),
        compiler_params=_params(2),
        name="mix",
    )(x, ya, yb, yc, mg, merge_b, mod, w_rg, w_sb, w_fox, w_o)


def _block_diag(w):
    n, bd, _ = w.shape
    per = RG_GROUP // bd
    eye = jnp.eye(per, dtype=w.dtype)
    wg = w.reshape(n // per, per, bd, bd)
    return jnp.einsum("gpde,pq->gpdqe", wg, eye).reshape(n // per, per * bd, per * bd).astype(BF16)


def kernel(x, c, ffn1_norm, ffn1_w1, ffn1_w3, ffn1_w2, mix_norm, w_in, conv_w, conv_b, rg_wa, rg_ba, rg_wx, rg_bx, rg_lam, fox_bf, merge_b, w_rg, w_sb, w_fox, w_o, ffn2_norm, ffn2_w1, ffn2_w3, ffn2_w2, ada_w, ada_b, final_norm, final_ada_w, final_ada_b):
    bsz, slen, d = x.shape
    depth = ada_w.shape[0]
    tm = min(512, slen)
    tc = 512
    mods = _ada(c, ada_w, ada_b).reshape(depth, bsz, 9, d)
    fm = _ada(c, final_ada_w[None], final_ada_b[None]).reshape(bsz, 2, d)
    cuts = np.cumsum((RG_WIDTH, RG_WIDTH, 3 * ATT_WIDTH, 3 * ATT_WIDTH, N_HEADS)).tolist()
    row = lambda v: v.reshape(1, -1)
    for l in range(depth):
        mod = mods[l]
        x = _ffn(x, row(ffn1_norm[l]), mod, ffn1_w1[l].astype(BF16), ffn1_w3[l].astype(BF16), ffn1_w2[l].astype(BF16),
                 row(final_norm), fm, 0, False, tm)
        wl = w_in[l].astype(BF16)
        w_f = jnp.pad(wl[:, cuts[3] : cuts[4]], ((0, 0), (0, LANES - N_HEADS)))
        ws = (wl[:, : cuts[0]], wl[:, cuts[0] : cuts[1]], wl[:, cuts[1] : cuts[2]], wl[:, cuts[2] : cuts[3]], w_f, wl[:, cuts[4] :])
        rgx, rgg, sb_qkv, fox_qkv, f, mg = _inproj(x, row(mix_norm[l]), mod, ws, tm)
        cum_col, cum_row = _fcum(f, jnp.pad(row(fox_bf[l]), ((0, 0), (0, LANES - N_HEADS))))
        cum_row = cum_row[:, :N_HEADS].reshape(bsz, N_HEADS, slen // ATT_BLOCK, 1, ATT_BLOCK)
        ya = _rglru(rgx, rgg, conv_w[l], row(conv_b[l]), _block_diag(rg_wa[l]), row(rg_ba[l]), _block_diag(rg_wx[l]),
                    row(rg_bx[l]), row(rg_lam[l]), tc)
        yb = _sb_attention(sb_qkv)
        yc = _fox_attention(fox_qkv, cum_col, cum_row)
        x = _mix(x, ya, yb, yc, mg, row(merge_b[l]), mod, w_rg[l].astype(BF16), w_sb[l].astype(BF16),
                 w_fox[l].astype(BF16), w_o[l].astype(BF16), tm)
        x = _ffn(x, row(ffn2_norm[l]), mod, ffn2_w1[l].astype(BF16), ffn2_w3[l].astype(BF16), ffn2_w2[l].astype(BF16),
                 row(final_norm), fm, 2, l == depth - 1, tm)
    return x
```
